```python
import math
import jax
import jax.numpy as jnp
from jax import lax
import numpy as np

D_MODEL = 1024
BATCH = 2
SEQ = 16384
DEPTH = 2

D_SSM = 512
SSM_GROUP = 16
N_SSM_GROUPS = D_SSM // SSM_GROUP
SSM_STATE = 64
D_ATTN = D_MODEL - D_SSM
HEAD_DIM = 64
N_Q_HEADS = D_ATTN // HEAD_DIM
N_KV_HEADS = 2
Q_PER_KV = N_Q_HEADS // N_KV_HEADS
KV_DIM = N_KV_HEADS * HEAD_DIM
D_IN = D_SSM + D_ATTN + 2 * KV_DIM
WINDOW = 128
BLOCK = 128
N_BUCKETS = 32
MAX_DISTANCE = 128
D_FF = 2816
CONV_WIDTH = 3
N_MOD = 6
EPS = 1e-6
NEG_INF = -1e30

kernel_name = "hymba_s5_swa_sink_convffn_adaln"


def _rms(x, w):
    xf = x.astype(jnp.float32)
    y = xf * lax.rsqrt(jnp.mean(xf * xf, axis=-1, keepdims=True) + EPS)
    return (y * w.astype(jnp.float32)).astype(x.dtype)


def _t5_bucket(n):
    n = np.maximum(n, 0)
    max_exact = N_BUCKETS // 2
    log_part = np.log(np.maximum(n, 1) / max_exact) / math.log(MAX_DISTANCE / max_exact)
    large = max_exact + (log_part * (N_BUCKETS - max_exact)).astype(np.int32)
    large = np.minimum(large, N_BUCKETS - 1)
    return np.where(n < max_exact, n, large).astype(np.int32)


def _s5(u, lam_re, lam_im, log_dt, b_re, b_im, c_re, c_im, d_skip, w_glu, b_glu):
    bsz, seq = u.shape[0], u.shape[1]
    f32 = jnp.float32
    ug = u.astype(f32).reshape(bsz, seq, N_SSM_GROUPS, SSM_GROUP)
    lr = jnp.minimum(lam_re.astype(f32), -1e-4)
    li = lam_im.astype(f32)
    dt = jnp.exp(log_dt.astype(f32))[:, None]
    mag = jnp.exp(dt * lr)
    a_re = mag * jnp.cos(dt * li)
    a_im = mag * jnp.sin(dt * li)
    den = lr * lr + li * li
    f_re = ((a_re - 1.0) * lr + a_im * li) / den
    f_im = (a_im * lr - (a_re - 1.0) * li) / den
    br = b_re.astype(f32)
    bi = b_im.astype(f32)
    bb_re = f_re[..., None] * br - f_im[..., None] * bi
    bb_im = f_re[..., None] * bi + f_im[..., None] * br
    bu_re = jnp.einsum("blgh,gph->blgp", ug, bb_re)
    bu_im = jnp.einsum("blgh,gph->blgp", ug, bb_im)
    shape_a = (1, seq, N_SSM_GROUPS, SSM_STATE)
    a_re_t = jnp.broadcast_to(a_re, shape_a)
    a_im_t = jnp.broadcast_to(a_im, shape_a)

    def combine(e1, e2):
        a1r, a1i, b1r, b1i = e1
        a2r, a2i, b2r, b2i = e2
        return (a2r * a1r - a2i * a1i,
                a2r * a1i + a2i * a1r,
                a2r * b1r - a2i * b1i + b2r,
                a2r * b1i + a2i * b1r + b2i)

    _, _, s_re, s_im = lax.associative_scan(combine, (a_re_t, a_im_t, bu_re, bu_im), axis=1)
    y = (jnp.einsum("blgp,ghp->blgh", s_re, c_re.astype(f32))
         - jnp.einsum("blgp,ghp->blgh", s_im, c_im.astype(f32))
         + d_skip.astype(f32).reshape(N_SSM_GROUPS, SSM_GROUP) * ug)
    z = jax.nn.gelu(y.reshape(bsz, seq, D_SSM))
    out = z * jax.nn.sigmoid(z @ w_glu.astype(f32) + b_glu.astype(f32))
    return out.astype(u.dtype)


def _swa_sink(q, k, v, rel_bias, sinks):
    bsz, seq = q.shape[0], q.shape[1]
    nb = seq // BLOCK
    f32 = jnp.float32
    qb = q.astype(f32).reshape(bsz, nb, BLOCK, N_KV_HEADS, Q_PER_KV, HEAD_DIM)
    pad = ((0, 0), (BLOCK, 0), (0, 0), (0, 0))
    kb = jnp.pad(k.astype(f32), pad).reshape(bsz, nb + 1, BLOCK, N_KV_HEADS, HEAD_DIM)
    vb = jnp.pad(v.astype(f32), pad).reshape(bsz, nb + 1, BLOCK, N_KV_HEADS, HEAD_DIM)
    k_band = jnp.concatenate([kb[:, :-1], kb[:, 1:]], axis=2)
    v_band = jnp.concatenate([vb[:, :-1], vb[:, 1:]], axis=2)
    logits = jnp.einsum("bnqhgd,bnkhd->bnhgqk", qb, k_band) * (HEAD_DIM ** -0.5)
    q_off = np.arange(BLOCK)[:, None] + BLOCK
    k_off = np.arange(2 * BLOCK)[None, :]
    dist = q_off - k_off
    bias = jnp.take(rel_bias.astype(f32), _t5_bucket(dist), axis=0)
    bias = bias.transpose(2, 0, 1).reshape(N_KV_HEADS, Q_PER_KV, BLOCK, 2 * BLOCK)
    key_pos = np.arange(nb)[:, None, None] * BLOCK + k_off[None] - BLOCK
    mask = (dist[None] >= 0) & (dist[None] < WINDOW) & (key_pos >= 0)
    logits = jnp.where(mask[None, :, None, None], logits + bias, NEG_INF)
    sink = sinks.astype(f32).reshape(N_KV_HEADS, Q_PER_KV)[None, None, :, :, None, None]
    m = jnp.maximum(jnp.max(logits, axis=-1, keepdims=True), sink)
    p = jnp.exp(logits - m)
    probs = p / (jnp.sum(p, axis=-1, keepdims=True) + jnp.exp(sink - m))
    out = jnp.einsum("bnhgqk,bnkhd->bnqhgd", probs, v_band)
    return out.reshape(bsz, seq, D_ATTN).astype(q.dtype)


def _conv_ffn(h, w_up, conv_w, conv_b, w_down):
    up = h @ w_up
    rhs = conv_w.reshape(CONV_WIDTH, 1, 2 * D_FF).astype(up.dtype)
    up = lax.conv_general_dilated(up, rhs, window_strides=(1,),
                                  padding=[(CONV_WIDTH - 1, 0)],
                                  dimension_numbers=("NWC", "WIO", "NWC"),
                                  feature_group_count=2 * D_FF) + conv_b
    val, gate = jnp.split(up, 2, axis=-1)
    return (jax.nn.silu(gate) * val) @ w_down


def setup_inputs(seed: int = 0) -> dict:
    key = jax.random.key(seed)
    ks = jax.random.split(key, 28)
    f32 = jnp.float32

    def nrm(k, shape, scale):
        return jax.random.normal(k, shape, f32) * scale

    def gain(k, shape):
        return 1.0 + 0.05 * jax.random.normal(k, shape, f32)

    G, P, H = N_SSM_GROUPS, SSM_STATE, SSM_GROUP
    lam_im0 = math.pi * jnp.arange(P, dtype=f32)
    return {
        "x": nrm(ks[0], (BATCH, SEQ, D_MODEL), 1.0),
        "c": nrm(ks[1], (BATCH, D_MODEL), 1.0),
        "w_mod": nrm(ks[2], (DEPTH, D_MODEL, N_MOD * D_MODEL), 0.5 * D_MODEL ** -0.5),
        "b_mod": nrm(ks[3], (DEPTH, N_MOD * D_MODEL), 0.02),
        "norm1_w": gain(ks[4], (DEPTH, D_MODEL)),
        "w_in": nrm(ks[5], (DEPTH, D_MODEL, D_IN), D_MODEL ** -0.5),
        "lam_re": -0.5 * jnp.exp(0.1 * jax.random.normal(ks[6], (DEPTH, G, P), f32)),
        "lam_im": lam_im0 + 0.05 * jax.random.normal(ks[7], (DEPTH, G, P), f32),
        "log_dt": jax.random.uniform(ks[8], (DEPTH, G), f32, math.log(1e-3), math.log(1e-1)),
        "ssm_b_re": nrm(ks[9], (DEPTH, G, P, H), (2 * H) ** -0.5),
        "ssm_b_im": nrm(ks[10], (DEPTH, G, P, H), (2 * H) ** -0.5),
        "ssm_c_re": nrm(ks[11], (DEPTH, G, H, P), (2 * P) ** -0.5),
        "ssm_c_im": nrm(ks[12], (DEPTH, G, H, P), (2 * P) ** -0.5),
        "ssm_d": nrm(ks[13], (DEPTH, D_SSM), 1.0),
        "w_glu": nrm(ks[14], (DEPTH, D_SSM, D_SSM), D_SSM ** -0.5),
        "b_glu": nrm(ks[15], (DEPTH, D_SSM), 0.02),
        "q_norm_w": gain(ks[16], (DEPTH, HEAD_DIM)),
        "k_norm_w": gain(ks[17], (DEPTH, HEAD_DIM)),
        "rel_bias": nrm(ks[18], (N_BUCKETS, N_Q_HEADS), 0.5),
        "sinks": nrm(ks[19], (DEPTH, N_Q_HEADS), 0.5),
        "out_norm_ssm": gain(ks[20], (DEPTH, D_SSM)),
        "out_norm_attn": gain(ks[21], (DEPTH, D_ATTN)),
        "w_out": nrm(ks[22], (DEPTH, D_MODEL, D_MODEL), D_MODEL ** -0.5),
        "norm2_w": gain(ks[23], (DEPTH, D_MODEL)),
        "w_up": nrm(ks[24], (DEPTH, D_MODEL, 2 * D_FF), D_MODEL ** -0.5),
        "conv_w": nrm(ks[25], (DEPTH, CONV_WIDTH, 2 * D_FF), CONV_WIDTH ** -0.5),
        "conv_b": nrm(ks[26], (DEPTH, 2 * D_FF), 0.02),
        "w_down": nrm(ks[27], (DEPTH, D_FF, D_MODEL), D_FF ** -0.5),
    }


def reference(x, c, w_mod, b_mod, norm1_w, w_in, lam_re, lam_im, log_dt,
              ssm_b_re, ssm_b_im, ssm_c_re, ssm_c_im, ssm_d, w_glu, b_glu,
              q_norm_w, k_norm_w, rel_bias, sinks, out_norm_ssm, out_norm_attn,
              w_out, norm2_w, w_up, conv_w, conv_b, w_down):
    bsz, seq = x.shape[0], x.shape[1]
    c_act = jax.nn.silu(c)
    for l in range(DEPTH):
        mod = (c_act @ w_mod[l] + b_mod[l])[:, None, :]
        sh1, sc1, g1, sh2, sc2, g2 = jnp.split(mod, N_MOD, axis=-1)

        h = _rms(x, norm1_w[l]) * (1 + sc1) + sh1
        proj = h @ w_in[l]
        u, q, k, v = jnp.split(proj, [D_SSM, D_SSM + D_ATTN, D_SSM + D_ATTN + KV_DIM], axis=-1)
        y_ssm = _s5(u, lam_re[l], lam_im[l], log_dt[l], ssm_b_re[l], ssm_b_im[l],
                    ssm_c_re[l], ssm_c_im[l], ssm_d[l], w_glu[l], b_glu[l])
        q = _rms(q.reshape(bsz, seq, N_Q_HEADS, HEAD_DIM), q_norm_w[l])
        k = _rms(k.reshape(bsz, seq, N_KV_HEADS, HEAD_DIM), k_norm_w[l])
        v = v.reshape(bsz, seq, N_KV_HEADS, HEAD_DIM)
        y_attn = _swa_sink(q, k, v, rel_bias, sinks[l])
        mixed = jnp.concatenate([_rms(y_ssm, out_norm_ssm[l]),
                                 _rms(y_attn, out_norm_attn[l])], axis=-1)
        x = x + g1 * (mixed @ w_out[l])

        h2 = _rms(x, norm2_w[l]) * (1 + sc2) + sh2
        x = x + g2 * _conv_ffn(h2, w_up[l], conv_w[l], conv_b[l], w_down[l])
    return x
```

```python
import functools
import math

import numpy as np
import jax
import jax.numpy as jnp
from jax import lax
from jax.experimental import pallas as pl
from jax.experimental.pallas import tpu as pltpu

D_MODEL = 1024
DEPTH = 2
D_SSM = 512
SSM_GROUP = 16
N_SSM_GROUPS = D_SSM // SSM_GROUP
SSM_STATE = 64
N_STATE = N_SSM_GROUPS * SSM_STATE
D_ATTN = D_MODEL - D_SSM
HEAD_DIM = 64
N_Q_HEADS = D_ATTN // HEAD_DIM
N_KV_HEADS = 2
Q_PER_KV = N_Q_HEADS // N_KV_HEADS
KV_DIM = N_KV_HEADS * HEAD_DIM
WINDOW = 128
BLOCK = 128
N_BUCKETS = 32
MAX_DISTANCE = 128
D_FF = 2816
N_MOD = 6
EPS = 1e-6
NEG_INF = -1e30

SUBLANES = 8
LANES = 128
VMEM_LIMIT_BYTES = 56 * 1024 * 1024

TM_IN = 512
T_S5 = 512
SEG = T_S5 // SUBLANES
LW_S5 = 512
TQ_ATTN = 512
TM_FFN = 512
TF_FFN = 256
N_FF_CHUNKS = D_FF // TF_FFN
D_PROJ = D_SSM + D_ATTN + 4 * KV_DIM

F32 = jnp.float32
BF16 = jnp.bfloat16


def _params(semantics):
    return pltpu.CompilerParams(dimension_semantics=semantics, vmem_limit_bytes=VMEM_LIMIT_BYTES)


def _const_spec(shape):
    nd = len(shape)
    return pl.BlockSpec(shape, lambda *_: (0,) * nd, pipeline_mode=pl.Buffered(1))


def _mod_kernel(c_ref, w_ref, b_ref, o_ref):
    c = c_ref[...]
    c_act = c * (1.0 / (1.0 + jnp.exp(-c)))
    o_ref[...] = jnp.dot(c_act, w_ref[...], preferred_element_type=F32) + b_ref[...]


def _modulation(c, w_mod, b_mod):
    bsz = c.shape[0]
    tn = 1536
    n = N_MOD * D_MODEL
    return pl.pallas_call(
        _mod_kernel,
        out_shape=jax.ShapeDtypeStruct((DEPTH, bsz, n), F32),
        grid=(DEPTH, n // tn),
        in_specs=[
            pl.BlockSpec((bsz, D_MODEL), lambda l, j: (0, 0)),
            pl.BlockSpec((None, D_MODEL, tn), lambda l, j: (l, 0, j)),
            pl.BlockSpec((None, 1, tn), lambda l, j: (l, 0, j)),
        ],
        out_specs=pl.BlockSpec((None, bsz, tn), lambda l, j: (l, 0, j)),
        compiler_params=_params(("arbitrary", "arbitrary")),
        name="adaln_mod",
    )(c, w_mod, b_mod.reshape(DEPTH, 1, n))


def _t5_bucket(n):
    n = np.maximum(n, 0)
    max_exact = N_BUCKETS // 2
    log_part = np.log(np.maximum(n, 1) / max_exact) / math.log(MAX_DISTANCE / max_exact)
    large = max_exact + (log_part * (N_BUCKETS - max_exact)).astype(np.int32)
    large = np.minimum(large, N_BUCKETS - 1)
    return np.where(n < max_exact, n, large).astype(np.int32)


def _band_buckets():
    dist = (np.arange(BLOCK)[:, None] + BLOCK) - np.arange(2 * BLOCK)[None, :]
    valid = (dist >= 0) & (dist < WINDOW)
    return np.where(valid, _t5_bucket(dist), -1).astype(np.int32)


def _bias_kernel(rb_ref, bucket_ref, o_ref):
    h = pl.program_id(0)
    bucket = bucket_ref[...]
    acc = jnp.full(bucket.shape, NEG_INF, F32)
    for b in range(N_BUCKETS):
        acc = jnp.where(bucket == b, rb_ref[b, h], acc)
    o_ref[...] = acc


def _bias_table(rel_bias):
    return pl.pallas_call(
        _bias_kernel,
        out_shape=jax.ShapeDtypeStruct((N_Q_HEADS, BLOCK, 2 * BLOCK), F32),
        grid=(N_Q_HEADS,),
        in_specs=[
            pl.BlockSpec(memory_space=pltpu.SMEM),
            pl.BlockSpec((BLOCK, 2 * BLOCK), lambda h: (0, 0)),
        ],
        out_specs=pl.BlockSpec((None, BLOCK, 2 * BLOCK), lambda h: (h, 0, 0)),
        compiler_params=_params(("arbitrary",)),
        name="rel_bias_table",
    )(rel_bias, jnp.asarray(_band_buckets()))


def _in_kernel(x_ref, mod_ref, n1w_ref, w_ref, segq_ref, segk_ref, qw_ref, kw_ref,
               u_ref, q_ref, k_ref, v_ref):
    x = x_ref[...]
    sh1 = mod_ref[0:1, :]
    sc1 = mod_ref[1:2, :]
    ms = jnp.mean(x * x, axis=-1, keepdims=True)
    h = (x * lax.rsqrt(ms + EPS) * n1w_ref[...]) * (1.0 + sc1) + sh1
    proj = jnp.dot(h.astype(BF16), w_ref[...], preferred_element_type=F32)
    u_ref[...] = proj[:, :D_SSM].astype(BF16)
    q = proj[:, D_SSM:D_SSM + D_ATTN]
    k = proj[:, D_SSM + D_ATTN:D_SSM + D_ATTN + 2 * KV_DIM]
    v = proj[:, D_SSM + D_ATTN + 2 * KV_DIM:]
    q_ms = jnp.dot((q * q).astype(BF16), segq_ref[...], preferred_element_type=F32) * (1.0 / HEAD_DIM)
    k_ms = jnp.dot((k * k).astype(BF16), segk_ref[...], preferred_element_type=F32) * (1.0 / HEAD_DIM)
    q_ref[...] = (q * lax.rsqrt(q_ms + EPS) * qw_ref[...]).astype(BF16)
    k_ref[...] = (k * lax.rsqrt(k_ms + EPS) * kw_ref[...]).astype(BF16)
    v_ref[...] = v.astype(BF16)


def _seg_ones(n):
    idx = np.arange(n) // HEAD_DIM
    return jnp.asarray((idx[:, None] == idx[None, :]).astype(np.float32), dtype=BF16)


def _in_projection(x2d, mod_l, n1w, w_ext, qw_ext, kw_ext, tiles_per_batch):
    ntok = x2d.shape[0]
    tm = TM_IN
    return pl.pallas_call(
        _in_kernel,
        out_shape=(
            jax.ShapeDtypeStruct((ntok, D_SSM), BF16),
            jax.ShapeDtypeStruct((ntok, D_ATTN), BF16),
            jax.ShapeDtypeStruct((ntok, 2 * KV_DIM), BF16),
            jax.ShapeDtypeStruct((ntok, 2 * KV_DIM), BF16),
        ),
        grid=(ntok // tm,),
        in_specs=[
            pl.BlockSpec((tm, D_MODEL), lambda i: (i, 0)),
            pl.BlockSpec((None, N_MOD, D_MODEL), lambda i: (i // tiles_per_batch, 0, 0)),
            _const_spec((1, D_MODEL)),
            _const_spec((D_MODEL, D_PROJ)),
            _const_spec((D_ATTN, D_ATTN)),
            _const_spec((2 * KV_DIM, 2 * KV_DIM)),
            _const_spec((1, D_ATTN)),
            _const_spec((1, 2 * KV_DIM)),
        ],
        out_specs=(
            pl.BlockSpec((tm, D_SSM), lambda i: (i, 0)),
            pl.BlockSpec((tm, D_ATTN), lambda i: (i, 0)),
            pl.BlockSpec((tm, 2 * KV_DIM), lambda i: (i, 0)),
            pl.BlockSpec((tm, 2 * KV_DIM), lambda i: (i, 0)),
        ),
        compiler_params=_params(("arbitrary",)),
        name="in_projection",
    )(x2d, mod_l, n1w, w_ext, _seg_ones(D_ATTN), _seg_ones(2 * KV_DIM), qw_ext, kw_ext)


def _cmul(ar, ai, br, bi):
    return ar * br - ai * bi, ar * bi + ai * br


def _s5_kernel(u_ref, perm_ref, permt_ref, b_ref, c_ref, d_ref, a_ref, g_ref,
               wglu_ref, bglu_ref, nw_ref, o_ref, sre_ref, sim_ref, carry_ref):
    @pl.when(pl.program_id(1) == 0)
    def _():
        carry_ref[...] = jnp.zeros_like(carry_ref)

    up = jnp.dot(perm_ref[...], u_ref[...], preferred_element_type=F32).astype(BF16)
    half = D_SSM // 2
    nh = N_STATE // 2
    for k in range(2):
        bu = jnp.dot(up[:, k * half:(k + 1) * half], b_ref[k], preferred_element_type=F32)
        sre_ref[:, k * nh:(k + 1) * nh] = bu[:, :nh]
        sim_ref[:, k * nh:(k + 1) * nh] = bu[:, nh:]

    row = lax.broadcasted_iota(jnp.int32, (SUBLANES, LW_S5), 0)
    for lb in range(N_STATE // LW_S5):
        lanes = slice(lb * LW_S5, (lb + 1) * LW_S5)
        ar = a_ref[0, :, lanes]
        ai = a_ref[1, :, lanes]

        def local_scan(r, st):
            rows = pl.ds(pl.multiple_of(r * SUBLANES, SUBLANES), SUBLANES)
            pr, pi = _cmul(ar, ai, st[0], st[1])
            nr = pr + sre_ref[rows, lanes]
            ni = pi + sim_ref[rows, lanes]
            sre_ref[rows, lanes] = nr
            sim_ref[rows, lanes] = ni
            return nr, ni

        zero = jnp.zeros((SUBLANES, LW_S5), F32)
        er, ei = lax.fori_loop(0, SEG, local_scan, (zero, zero), unroll=4)

        cin_r = jnp.broadcast_to(carry_ref[0:1, lanes], (SUBLANES, LW_S5))
        cin_i = jnp.broadcast_to(carry_ref[1:2, lanes], (SUBLANES, LW_S5))
        cr = jnp.where(row == 0, cin_r, pltpu.roll(er, 1, axis=0))
        ci = jnp.where(row == 0, cin_i, pltpu.roll(ei, 1, axis=0))
        for step, shift in enumerate((1, 2, 4)):
            gr = g_ref[2 * step, :, lanes]
            gi = g_ref[2 * step + 1, :, lanes]
            tr, ti = _cmul(gr, gi, pltpu.roll(cr, shift, axis=0), pltpu.roll(ci, shift, axis=0))
            cr = cr + tr
            ci = ci + ti
        nxt_r, nxt_i = _cmul(g_ref[6, :, lanes], g_ref[7, :, lanes], cr, ci)
        carry_ref[0:1, lanes] = (nxt_r + er)[SUBLANES - 1:SUBLANES, :]
        carry_ref[1:2, lanes] = (nxt_i + ei)[SUBLANES - 1:SUBLANES, :]

        def add_carry(r, q):
            rows = pl.ds(pl.multiple_of(r * SUBLANES, SUBLANES), SUBLANES)
            qr, qi = _cmul(ar, ai, q[0], q[1])
            sre_ref[rows, lanes] = sre_ref[rows, lanes] + qr
            sim_ref[rows, lanes] = sim_ref[rows, lanes] + qi
            return qr, qi

        lax.fori_loop(0, SEG, add_carry, (cr, ci), unroll=4)

    ys = []
    for j in range(2):
        lanes = slice(j * nh, (j + 1) * nh)
        y = jnp.dot(sre_ref[:, lanes].astype(BF16), c_ref[j, 0], preferred_element_type=F32)
        y = y + jnp.dot(sim_ref[:, lanes].astype(BF16), c_ref[j, 1], preferred_element_type=F32)
        ys.append(y)
    y = jnp.concatenate(ys, axis=1) + d_ref[...] * up.astype(F32)
    z = jax.nn.gelu(y)
    gate = jnp.dot(z.astype(BF16), wglu_ref[...], preferred_element_type=F32) + bglu_ref[...]
    out = z * (1.0 / (1.0 + jnp.exp(-gate)))
    ms = jnp.mean(out * out, axis=-1, keepdims=True)
    outn = (out * lax.rsqrt(ms + EPS) * nw_ref[...]).astype(BF16)
    o_ref[...] = jnp.dot(permt_ref[...], outn, preferred_element_type=F32).astype(BF16)


def _segment_perm():
    p = np.zeros((T_S5, T_S5), np.float32)
    for r in range(SEG):
        for s in range(SUBLANES):
            p[r * SUBLANES + s, s * SEG + r] = 1.0
    return p


def _s5_operands(lam_re, lam_im, log_dt, b_re, b_im, c_re, c_im):
    lr = jnp.minimum(lam_re.astype(F32), -1e-4)
    li = lam_im.astype(F32)
    dt = jnp.exp(log_dt.astype(F32))[:, None]
    mag = jnp.exp(dt * lr)
    a_re = mag * jnp.cos(dt * li)
    a_im = mag * jnp.sin(dt * li)
    den = lr * lr + li * li
    f_re = ((a_re - 1.0) * lr + a_im * li) / den
    f_im = (a_im * lr - (a_re - 1.0) * li) / den
    br = b_re.astype(F32)
    bi = b_im.astype(F32)
    bb_re = f_re[..., None] * br - f_im[..., None] * bi
    bb_im = f_re[..., None] * bi + f_im[..., None] * br

    eye = jnp.eye(N_SSM_GROUPS // 2, dtype=F32)

    def in_blocks(bb):
        x = bb.reshape(2, N_SSM_GROUPS // 2, SSM_STATE, SSM_GROUP)
        return jnp.einsum("kgph,gq->kghqp", x, eye).reshape(2, D_SSM // 2, N_STATE // 2)

    def out_blocks(cc):
        x = cc.astype(F32).reshape(2, N_SSM_GROUPS // 2, SSM_GROUP, SSM_STATE)
        return jnp.einsum("jghp,gq->jgpqh", x, eye).reshape(2, N_STATE // 2, D_SSM // 2)

    b_mat = jnp.concatenate([in_blocks(bb_re), in_blocks(bb_im)], axis=-1).astype(BF16)
    c_mat = jnp.stack([out_blocks(c_re), -out_blocks(c_im)], axis=1).astype(BF16)

    def power(n):
        m = jnp.exp(n * (dt * lr))
        return (m * jnp.cos(n * (dt * li))).reshape(1, N_STATE), (m * jnp.sin(n * (dt * li))).reshape(1, N_STATE)

    def tiled(v, zero_rows=0):
        t = jnp.broadcast_to(v, (SUBLANES, N_STATE))
        keep = (np.arange(SUBLANES) >= zero_rows)[:, None]
        return jnp.where(keep, t, 0.0)

    a_tab = jnp.stack([tiled(a_re.reshape(1, N_STATE)), tiled(a_im.reshape(1, N_STATE))])
    g_rows = []
    for shift in (1, 2, 4):
        pr, pi = power(float(shift * SEG))
        g_rows += [tiled(pr, shift), tiled(pi, shift)]
    pr, pi = power(float(SEG))
    g_rows += [tiled(pr), tiled(pi)]
    return b_mat, c_mat, a_tab, jnp.stack(g_rows)


def _s5(u, ops, d_skip, w_glu, b_glu, nw, bsz, seq):
    b_mat, c_mat, a_tab, g_tab = ops
    t = T_S5
    tiles = seq // t
    perm = _segment_perm()
    return pl.pallas_call(
        _s5_kernel,
        out_shape=jax.ShapeDtypeStruct((bsz * seq, D_SSM), BF16),
        grid=(bsz, tiles),
        in_specs=[
            pl.BlockSpec((t, D_SSM), lambda b, i: (b * tiles + i, 0)),
            _const_spec((t, t)),
            _const_spec((t, t)),
            _const_spec(b_mat.shape),
            _const_spec(c_mat.shape),
            _const_spec((1, D_SSM)),
            _const_spec(a_tab.shape),
            _const_spec(g_tab.shape),
            _const_spec((D_SSM, D_SSM)),
            _const_spec((1, D_SSM)),
            _const_spec((1, D_SSM)),
        ],
        out_specs=pl.BlockSpec((t, D_SSM), lambda b, i: (b * tiles + i, 0)),
        scratch_shapes=[
            pltpu.VMEM((t, N_STATE), F32),
            pltpu.VMEM((t, N_STATE), F32),
            pltpu.VMEM((2, N_STATE), F32),
        ],
        compiler_params=_params(("arbitrary", "arbitrary")),
        name="s5_scan",
    )(u, jnp.asarray(perm, dtype=BF16), jnp.asarray(perm.T, dtype=BF16), b_mat, c_mat,
      d_skip.reshape(1, D_SSM).astype(F32), a_tab, g_tab, w_glu.astype(BF16),
      b_glu.reshape(1, D_SSM).astype(F32), nw.reshape(1, D_SSM).astype(F32))


def _attn_kernel(sink_ref, q_ref, kc_ref, kp_ref, vc_ref, vp_ref, bias_ref, nw_ref, o_ref):
    first = pl.program_id(1) == 0
    lane = lax.broadcasted_iota(jnp.int32, (BLOCK, 2 * HEAD_DIM), 1)
    low = lane < HEAD_DIM
    col = lax.broadcasted_iota(jnp.int32, (Q_PER_KV * BLOCK, 2 * BLOCK), 1)
    start_mask = jnp.where(jnp.logical_and(first, col < BLOCK), NEG_INF, 0.0)
    ones = jnp.ones((2 * BLOCK, 2 * HEAD_DIM), BF16)
    zero = jnp.zeros((BLOCK, 2 * HEAD_DIM), BF16)
    for jb in range(TQ_ATTN // BLOCK):
        qb = q_ref[jb * BLOCK:(jb + 1) * BLOCK, :]
        pairs = []
        for g in range(N_KV_HEADS):
            kv_lanes = slice(g * 2 * HEAD_DIM, (g + 1) * 2 * HEAD_DIM)
            if jb == 0:
                kb = jnp.concatenate([kp_ref[:, kv_lanes], kc_ref[0:BLOCK, kv_lanes]], axis=0)
                vb = jnp.concatenate([vp_ref[:, kv_lanes], vc_ref[0:BLOCK, kv_lanes]], axis=0)
            else:
                kb = kc_ref[(jb - 1) * BLOCK:(jb + 1) * BLOCK, kv_lanes]
                vb = vc_ref[(jb - 1) * BLOCK:(jb + 1) * BLOCK, kv_lanes]
            stack = []
            for m in range(Q_PER_KV // 2):
                qp = qb[:, (g * 2 + m) * 2 * HEAD_DIM:(g * 2 + m + 1) * 2 * HEAD_DIM]
                stack += [jnp.where(low, qp, zero), jnp.where(low, zero, qp)]
            q4 = jnp.concatenate(stack, axis=0)
            lg = lax.dot_general(q4, kb, (((1,), (1,)), ((), ())), preferred_element_type=F32)
            lg = lg + bias_ref[g * Q_PER_KV:(g + 1) * Q_PER_KV].reshape(Q_PER_KV * BLOCK, 2 * BLOCK)
            if jb == 0:
                lg = lg + start_mask
            sink = jnp.concatenate(
                [jnp.full((BLOCK, 1), sink_ref[g * Q_PER_KV + e], F32) for e in range(Q_PER_KV)], axis=0)
            mx = jnp.maximum(jnp.max(lg, axis=-1, keepdims=True), sink)
            p = jnp.exp(lg - mx).astype(BF16)
            pv = jnp.dot(p, jnp.concatenate([vb, ones], axis=1), preferred_element_type=F32)
            o4 = pv[:, :2 * HEAD_DIM] / (pv[:, 2 * HEAD_DIM:] + jnp.exp(sink - mx))
            for m in range(Q_PER_KV // 2):
                pairs.append(jnp.where(low, o4[(2 * m) * BLOCK:(2 * m + 1) * BLOCK],
                                       o4[(2 * m + 1) * BLOCK:(2 * m + 2) * BLOCK]))
        y = jnp.concatenate(pairs, axis=1)
        ms = jnp.mean(y * y, axis=-1, keepdims=True)
        o_ref[jb * BLOCK:(jb + 1) * BLOCK, :] = (y * lax.rsqrt(ms + EPS) * nw_ref[...]).astype(BF16)


def _attention(q, kd, vd, bias, sinks, nw, bsz, seq):
    tq = TQ_ATTN
    tiles = seq // tq
    ratio = tq // BLOCK

    def prev_map(b, i):
        return (jnp.maximum((b * tiles + i) * ratio - 1, 0), 0)

    return pl.pallas_call(
        _attn_kernel,
        out_shape=jax.ShapeDtypeStruct((bsz * seq, D_ATTN), BF16),
        grid=(bsz, tiles),
        in_specs=[
            pl.BlockSpec(memory_space=pltpu.SMEM),
            pl.BlockSpec((tq, D_ATTN), lambda b, i: (b * tiles + i, 0)),
            pl.BlockSpec((tq, 2 * KV_DIM), lambda b, i: (b * tiles + i, 0)),
            pl.BlockSpec((BLOCK, 2 * KV_DIM), prev_map),
            pl.BlockSpec((tq, 2 * KV_DIM), lambda b, i: (b * tiles + i, 0)),
            pl.BlockSpec((BLOCK, 2 * KV_DIM), prev_map),
            _const_spec((N_Q_HEADS, BLOCK, 2 * BLOCK)),
            _const_spec((1, D_ATTN)),
        ],
        out_specs=pl.BlockSpec((tq, D_ATTN), lambda b, i: (b * tiles + i, 0)),
        compiler_params=_params(("arbitrary", "arbitrary")),
        name="swa_attention",
    )(sinks.astype(F32), q, kd, kd, vd, vd, bias, nw.reshape(1, D_ATTN).astype(F32))


def _ffn_kernel(x_ref, ys_ref, ya_ref, mod_ref, wos_ref, woa_ref, n2w_ref, wv_ref, wg_ref,
                cwv_ref, cwg_ref, cbv_ref, cbg_ref, wd_ref, o_ref, h_ref, acc_ref, tail_ref,
                *, tiles_per_batch):
    first = (pl.program_id(0) % tiles_per_batch) == 0
    g1 = mod_ref[2:3, :]
    sh2 = mod_ref[3:4, :]
    sc2 = mod_ref[4:5, :]
    g2 = mod_ref[5:6, :]
    mixed = jnp.dot(ys_ref[...], wos_ref[...], preferred_element_type=F32)
    mixed = mixed + jnp.dot(ya_ref[...], woa_ref[...], preferred_element_type=F32)
    x1 = x_ref[...] + g1 * mixed
    ms = jnp.mean(x1 * x1, axis=-1, keepdims=True)
    h_ref[...] = ((x1 * lax.rsqrt(ms + EPS) * n2w_ref[...]) * (1.0 + sc2) + sh2).astype(BF16)
    acc_ref[...] = jnp.zeros_like(acc_ref)
    tm = x_ref.shape[0]

    def conv(up, cw, cb, tail_slot):
        prev = jnp.where(first, 0.0, tail_ref[tail_slot])
        tail_ref[tail_slot] = up[tm - SUBLANES:, :]
        ext = jnp.concatenate([prev, up], axis=0)
        up1 = pltpu.roll(ext, 1, axis=0)[SUBLANES:, :]
        up2 = pltpu.roll(ext, 2, axis=0)[SUBLANES:, :]
        return cw[0:1, :] * up2 + cw[1:2, :] * up1 + cw[2:3, :] * up + cb

    def chunk(j, carry):
        h = h_ref[...]
        val = conv(jnp.dot(h, wv_ref[j], preferred_element_type=F32), cwv_ref[j], cbv_ref[j], 2 * j)
        gate = conv(jnp.dot(h, wg_ref[j], preferred_element_type=F32), cwg_ref[j], cbg_ref[j], 2 * j + 1)
        act = (gate * (1.0 / (1.0 + jnp.exp(-gate))) * val).astype(BF16)
        acc_ref[...] += jnp.dot(act, wd_ref[j], preferred_element_type=F32)
        return carry

    lax.fori_loop(0, N_FF_CHUNKS, chunk, 0)
    o_ref[...] = x1 + g2 * acc_ref[...]


def _out_ffn(x2d, ys, ya, mod_l, w_out, n2w, w_up, conv_w, conv_b, w_down, tiles_per_batch):
    ntok = x2d.shape[0]
    tm = TM_FFN
    nc, tf = N_FF_CHUNKS, TF_FFN

    def cols(a):
        r = a.shape[0]
        v = a[:, :D_FF].reshape(r, nc, tf).transpose(1, 0, 2)
        g = a[:, D_FF:].reshape(r, nc, tf).transpose(1, 0, 2)
        return v, g

    wv, wg = cols(w_up.astype(BF16))
    cwv, cwg = cols(conv_w.astype(F32))
    cbv, cbg = cols(conv_b.reshape(1, 2 * D_FF).astype(F32))
    wd = w_down.astype(BF16).reshape(nc, tf, D_MODEL)
    wo = w_out.astype(BF16)
    return pl.pallas_call(
        functools.partial(_ffn_kernel, tiles_per_batch=tiles_per_batch),
        out_shape=jax.ShapeDtypeStruct((ntok, D_MODEL), F32),
        grid=(ntok // tm,),
        in_specs=[
            pl.BlockSpec((tm, D_MODEL), lambda i: (i, 0)),
            pl.BlockSpec((tm, D_SSM), lambda i: (i, 0)),
            pl.BlockSpec((tm, D_ATTN), lambda i: (i, 0)),
            pl.BlockSpec((None, N_MOD, D_MODEL), lambda i: (i // tiles_per_batch, 0, 0)),
            _const_spec((D_SSM, D_MODEL)),
            _const_spec((D_ATTN, D_MODEL)),
            _const_spec((1, D_MODEL)),
            _const_spec((nc, D_MODEL, tf)),
            _const_spec((nc, D_MODEL, tf)),
            _const_spec((nc, 3, tf)),
            _const_spec((nc, 3, tf)),
            _const_spec((nc, 1, tf)),
            _const_spec((nc, 1, tf)),
            _const_spec((nc, tf, D_MODEL)),
        ],
        out_specs=pl.BlockSpec((tm, D_MODEL), lambda i: (i, 0)),
        scratch_shapes=[
            pltpu.VMEM((tm, D_MODEL), BF16),
            pltpu.VMEM((tm, D_MODEL), F32),
            pltpu.VMEM((2 * nc, SUBLANES, tf), F32),
        ],
        compiler_params=_params(("arbitrary",)),
        name="out_proj_convffn",
    )(x2d, ys, ya, mod_l, wo[:D_SSM], wo[D_SSM:], n2w.reshape(1, D_MODEL).astype(F32),
      wv, wg, cwv, cwg, cbv, cbg, wd)


def _in_weights(w_in_l, qw, kw):
    wu = w_in_l[:, :D_SSM]
    wq = w_in_l[:, D_SSM:D_SSM + D_ATTN]
    wk = w_in_l[:, D_SSM + D_ATTN:D_SSM + D_ATTN + KV_DIM].reshape(D_MODEL, N_KV_HEADS, 1, HEAD_DIM)
    wv = w_in_l[:, D_SSM + D_ATTN + KV_DIM:].reshape(D_MODEL, N_KV_HEADS, 1, HEAD_DIM)
    wk2 = jnp.broadcast_to(wk, (D_MODEL, N_KV_HEADS, 2, HEAD_DIM)).reshape(D_MODEL, 2 * KV_DIM)
    wv2 = jnp.broadcast_to(wv, (D_MODEL, N_KV_HEADS, 2, HEAD_DIM)).reshape(D_MODEL, 2 * KV_DIM)
    w_ext = jnp.concatenate([wu, wq, wk2, wv2], axis=1).astype(BF16)
    qw_ext = (jnp.tile(qw.astype(F32), N_Q_HEADS) * (HEAD_DIM ** -0.5)).reshape(1, D_ATTN)
    kw_ext = jnp.tile(kw.astype(F32), 2 * N_KV_HEADS).reshape(1, 2 * KV_DIM)
    return w_ext, qw_ext, kw_ext


def kernel(x, c, w_mod, b_mod, norm1_w, w_in, lam_re, lam_im, log_dt, ssm_b_re, ssm_b_im, ssm_c_re,
           ssm_c_im, ssm_d, w_glu, b_glu, q_norm_w, k_norm_w, rel_bias, sinks, out_norm_ssm,
           out_norm_attn, w_out, norm2_w, w_up, conv_w, conv_b, w_down):
    bsz, seq, _ = x.shape
    assert seq % T_S5 == 0 and seq % TQ_ATTN == 0 and seq % TM_IN == 0 and seq % TM_FFN == 0
    mod = _modulation(c.astype(F32), w_mod.astype(F32), b_mod.astype(F32))
    mod = mod.reshape(DEPTH, bsz, N_MOD, D_MODEL)
    bias = _bias_table(rel_bias.astype(F32))
    x2d = x.reshape(bsz * seq, D_MODEL).astype(F32)
    for l in range(DEPTH):
        w_ext, qw_ext, kw_ext = _in_weights(w_in[l], q_norm_w[l], k_norm_w[l])
        u, q, kd, vd = _in_projection(x2d, mod[l], norm1_w[l].reshape(1, D_MODEL).astype(F32),
                                      w_ext, qw_ext, kw_ext, seq // TM_IN)
        ops = _s5_operands(lam_re[l], lam_im[l], log_dt[l], ssm_b_re[l], ssm_b_im[l],
                           ssm_c_re[l], ssm_c_im[l])
        ys = _s5(u, ops, ssm_d[l], w_glu[l], b_glu[l], out_norm_ssm[l], bsz, seq)
        ya = _attention(q, kd, vd, bias, sinks[l], out_norm_attn[l], bsz, seq)
        x2d = _out_ffn(x2d, ys, ya, mod[l], w_out[l], norm2_w[l], w_up[l], conv_w[l], conv_b[l],
                       w_down[l], seq // TM_FFN)
    return x2d.reshape(bsz, seq, D_MODEL).astype(x.dtype)
```

```python
import functools
import math

import numpy as np
import jax
import jax.numpy as jnp
from jax import lax
from jax.experimental import pallas as pl
from jax.experimental.pallas import tpu as pltpu

D_MODEL = 1024
DEPTH = 2
D_SSM = 512
SSM_GROUP = 16
N_SSM_GROUPS = D_SSM // SSM_GROUP
SSM_STATE = 64
N_STATE = N_SSM_GROUPS * SSM_STATE
D_ATTN = D_MODEL - D_SSM
HEAD_DIM = 64
N_Q_HEADS = D_ATTN // HEAD_DIM
N_KV_HEADS = 2
Q_PER_KV = N_Q_HEADS // N_KV_HEADS
KV_DIM = N_KV_HEADS * HEAD_DIM
WINDOW = 128
BLOCK = 128
N_BUCKETS = 32
MAX_DISTANCE = 128
D_FF = 2816
N_MOD = 6
EPS = 1e-6
NEG_INF = -1e30

SUBLANES = 8
LANES = 128
VMEM_LIMIT_BYTES = 56 * 1024 * 1024

TM_IN = 512
T_S5 = 512
SEG = T_S5 // SUBLANES
LW_S5 = 512
TQ_ATTN = 512
TM_FFN = 512
TF_FFN = 256
N_FF_CHUNKS = D_FF // TF_FFN
D_PROJ = D_SSM + D_ATTN + 4 * KV_DIM

F32 = jnp.float32
BF16 = jnp.bfloat16


def _params(semantics):
    return pltpu.CompilerParams(dimension_semantics=semantics, vmem_limit_bytes=VMEM_LIMIT_BYTES)


def _const_spec(shape):
    nd = len(shape)
    return pl.BlockSpec(shape, lambda *_: (0,) * nd, pipeline_mode=pl.Buffered(1))


def _mod_kernel(c_ref, w_ref, b_ref, o_ref):
    c = c_ref[...]
    c_act = c * (1.0 / (1.0 + jnp.exp(-c)))
    o_ref[...] = jnp.dot(c_act, w_ref[...], preferred_element_type=F32) + b_ref[...]


def _modulation(c, w_mod, b_mod):
    bsz = c.shape[0]
    tn = 1536
    n = N_MOD * D_MODEL
    return pl.pallas_call(
        _mod_kernel,
        out_shape=jax.ShapeDtypeStruct((DEPTH, bsz, n), F32),
        grid=(DEPTH, n // tn),
        in_specs=[
            pl.BlockSpec((bsz, D_MODEL), lambda l, j: (0, 0)),
            pl.BlockSpec((None, D_MODEL, tn), lambda l, j: (l, 0, j)),
            pl.BlockSpec((None, 1, tn), lambda l, j: (l, 0, j)),
        ],
        out_specs=pl.BlockSpec((None, bsz, tn), lambda l, j: (l, 0, j)),
        compiler_params=_params(("arbitrary", "arbitrary")),
        name="adaln_mod",
    )(c, w_mod, b_mod.reshape(DEPTH, 1, n))


def _t5_bucket(n):
    n = np.maximum(n, 0)
    max_exact = N_BUCKETS // 2
    log_part = np.log(np.maximum(n, 1) / max_exact) / math.log(MAX_DISTANCE / max_exact)
    large = max_exact + (log_part * (N_BUCKETS - max_exact)).astype(np.int32)
    large = np.minimum(large, N_BUCKETS - 1)
    return np.where(n < max_exact, n, large).astype(np.int32)


def _band_buckets():
    dist = (np.arange(BLOCK)[:, None] + BLOCK) - np.arange(2 * BLOCK)[None, :]
    valid = (dist >= 0) & (dist < WINDOW)
    return np.where(valid, _t5_bucket(dist), -1).astype(np.int32)


def _bias_kernel(rb_ref, bucket_ref, o_ref):
    h = pl.program_id(0)
    bucket = bucket_ref[...]
    acc = jnp.full(bucket.shape, NEG_INF, F32)
    for b in range(N_BUCKETS):
        acc = jnp.where(bucket == b, rb_ref[b, h], acc)
    o_ref[...] = acc


def _bias_table(rel_bias):
    return pl.pallas_call(
        _bias_kernel,
        out_shape=jax.ShapeDtypeStruct((N_Q_HEADS, BLOCK, 2 * BLOCK), F32),
        grid=(N_Q_HEADS,),
        in_specs=[
            pl.BlockSpec(memory_space=pltpu.SMEM),
            pl.BlockSpec((BLOCK, 2 * BLOCK), lambda h: (0, 0)),
        ],
        out_specs=pl.BlockSpec((None, BLOCK, 2 * BLOCK), lambda h: (h, 0, 0)),
        compiler_params=_params(("arbitrary",)),
        name="rel_bias_table",
    )(rel_bias, jnp.asarray(_band_buckets()))


def _in_kernel(x_ref, mod_ref, n1w_ref, w_ref, segq_ref, segk_ref, qw_ref, kw_ref,
               u_ref, q_ref, k_ref, v_ref):
    x = x_ref[...]
    sh1 = mod_ref[0:1, :]
    sc1 = mod_ref[1:2, :]
    ms = jnp.mean(x * x, axis=-1, keepdims=True)
    h = (x * lax.rsqrt(ms + EPS) * n1w_ref[...]) * (1.0 + sc1) + sh1
    proj = jnp.dot(h.astype(BF16), w_ref[...], preferred_element_type=F32)
    u_ref[...] = proj[:, :D_SSM].astype(BF16)
    q = proj[:, D_SSM:D_SSM + D_ATTN]
    k = proj[:, D_SSM + D_ATTN:D_SSM + D_ATTN + 2 * KV_DIM]
    v = proj[:, D_SSM + D_ATTN + 2 * KV_DIM:]
    q_ms = jnp.dot((q * q).astype(BF16), segq_ref[...], preferred_element_type=F32) * (1.0 / HEAD_DIM)
    k_ms = jnp.dot((k * k).astype(BF16), segk_ref[...], preferred_element_type=F32) * (1.0 / HEAD_DIM)
    q_ref[...] = (q * lax.rsqrt(q_ms + EPS) * qw_ref[...]).astype(BF16)
    k_ref[...] = (k * lax.rsqrt(k_ms + EPS) * kw_ref[...]).astype(BF16)
    v_ref[...] = v.astype(BF16)


def _seg_ones(n):
    idx = np.arange(n) // HEAD_DIM
    return jnp.asarray((idx[:, None] == idx[None, :]).astype(np.float32), dtype=BF16)


def _in_projection(x2d, mod_l, n1w, w_ext, qw_ext, kw_ext, tiles_per_batch):
    ntok = x2d.shape[0]
    tm = TM_IN
    return pl.pallas_call(
        _in_kernel,
        out_shape=(
            jax.ShapeDtypeStruct((ntok, D_SSM), BF16),
            jax.ShapeDtypeStruct((ntok, D_ATTN), BF16),
            jax.ShapeDtypeStruct((ntok, 2 * KV_DIM), BF16),
            jax.ShapeDtypeStruct((ntok, 2 * KV_DIM), BF16),
        ),
        grid=(ntok // tm,),
        in_specs=[
            pl.BlockSpec((tm, D_MODEL), lambda i: (i, 0)),
            pl.BlockSpec((None, N_MOD, D_MODEL), lambda i: (i // tiles_per_batch, 0, 0)),
            _const_spec((1, D_MODEL)),
            _const_spec((D_MODEL, D_PROJ)),
            _const_spec((D_ATTN, D_ATTN)),
            _const_spec((2 * KV_DIM, 2 * KV_DIM)),
            _const_spec((1, D_ATTN)),
            _const_spec((1, 2 * KV_DIM)),
        ],
        out_specs=(
            pl.BlockSpec((tm, D_SSM), lambda i: (i, 0)),
            pl.BlockSpec((tm, D_ATTN), lambda i: (i, 0)),
            pl.BlockSpec((tm, 2 * KV_DIM), lambda i: (i, 0)),
            pl.BlockSpec((tm, 2 * KV_DIM), lambda i: (i, 0)),
        ),
        compiler_params=_params(("arbitrary",)),
        name="in_projection",
    )(x2d, mod_l, n1w, w_ext, _seg_ones(D_ATTN), _seg_ones(2 * KV_DIM), qw_ext, kw_ext)


def _cmul(ar, ai, br, bi):
    return ar * br - ai * bi, ar * bi + ai * br


def _s5_kernel(u_ref, perm_ref, permt_ref, b_ref, c_ref, d_ref, a_ref, g_ref,
               wglu_ref, bglu_ref, nw_ref, o_ref, sre_ref, sim_ref, sbre_ref, sbim_ref, carry_ref):
    @pl.when(pl.program_id(1) == 0)
    def _():
        carry_ref[...] = jnp.zeros_like(carry_ref)

    up = jnp.dot(perm_ref[...], u_ref[...], preferred_element_type=F32).astype(BF16)
    n_lb = N_STATE // LW_S5
    cw = D_SSM // n_lb
    row = lax.broadcasted_iota(jnp.int32, (SUBLANES, LW_S5), 0)

    def in_matmul(lb):
        lanes = slice(lb * LW_S5, (lb + 1) * LW_S5)
        bu = jnp.dot(up[:, lb * cw:(lb + 1) * cw], b_ref[lb], preferred_element_type=F32)
        sre_ref[:, lanes] = bu[:, :LW_S5]
        sim_ref[:, lanes] = bu[:, LW_S5:]

    def scan(lb):
        lanes = slice(lb * LW_S5, (lb + 1) * LW_S5)
        ar = a_ref[0, :, lanes]
        ai = a_ref[1, :, lanes]
        er = jnp.zeros((SUBLANES, LW_S5), F32)
        ei = er
        for r in range(SEG):
            rows = slice(r * SUBLANES, (r + 1) * SUBLANES)
            pr, pi = _cmul(ar, ai, er, ei)
            er = pr + sre_ref[rows, lanes]
            ei = pi + sim_ref[rows, lanes]
            sre_ref[rows, lanes] = er
            sim_ref[rows, lanes] = ei
        cin_r = jnp.broadcast_to(carry_ref[0:1, lanes], (SUBLANES, LW_S5))
        cin_i = jnp.broadcast_to(carry_ref[1:2, lanes], (SUBLANES, LW_S5))
        cr = jnp.where(row == 0, cin_r, pltpu.roll(er, 1, axis=0))
        ci = jnp.where(row == 0, cin_i, pltpu.roll(ei, 1, axis=0))
        for step, shift in enumerate((1, 2, 4)):
            gr = g_ref[2 * step, :, lanes]
            gi = g_ref[2 * step + 1, :, lanes]
            tr, ti = _cmul(gr, gi, pltpu.roll(cr, shift, axis=0), pltpu.roll(ci, shift, axis=0))
            cr = cr + tr
            ci = ci + ti
        nxt_r, nxt_i = _cmul(g_ref[6, :, lanes], g_ref[7, :, lanes], cr, ci)
        carry_ref[0:1, lanes] = (nxt_r + er)[SUBLANES - 1:SUBLANES, :]
        carry_ref[1:2, lanes] = (nxt_i + ei)[SUBLANES - 1:SUBLANES, :]
        qr, qi = cr, ci
        for r2 in range(SEG // 2):
            vr, vi = [], []
            for r in (2 * r2, 2 * r2 + 1):
                rows = slice(r * SUBLANES, (r + 1) * SUBLANES)
                qr, qi = _cmul(ar, ai, qr, qi)
                vr.append(sre_ref[rows, lanes] + qr)
                vi.append(sim_ref[rows, lanes] + qi)
            rows2 = slice(r2 * 2 * SUBLANES, (r2 + 1) * 2 * SUBLANES)
            sbre_ref[rows2, lanes] = jnp.concatenate(vr, axis=0).astype(BF16)
            sbim_ref[rows2, lanes] = jnp.concatenate(vi, axis=0).astype(BF16)

    def out_matmul(lb):
        lanes = slice(lb * LW_S5, (lb + 1) * LW_S5)
        y = jnp.dot(sbre_ref[:, lanes], c_ref[lb, 0], preferred_element_type=F32)
        return y + jnp.dot(sbim_ref[:, lanes], c_ref[lb, 1], preferred_element_type=F32)

    ys = []
    in_matmul(0)
    for lb in range(n_lb):
        if lb + 1 < n_lb:
            in_matmul(lb + 1)
        scan(lb)
        ys.append(out_matmul(lb))
    y = jnp.concatenate(ys, axis=1) + d_ref[...] * up.astype(F32)
    z = jax.nn.gelu(y)
    gate = jnp.dot(z.astype(BF16), wglu_ref[...], preferred_element_type=F32) + bglu_ref[...]
    out = z * (1.0 / (1.0 + jnp.exp(-gate)))
    ms = jnp.mean(out * out, axis=-1, keepdims=True)
    outn = (out * lax.rsqrt(ms + EPS) * nw_ref[...]).astype(BF16)
    o_ref[...] = jnp.dot(permt_ref[...], outn, preferred_element_type=F32).astype(BF16)


def _segment_perm():
    p = np.zeros((T_S5, T_S5), np.float32)
    for r in range(SEG):
        for s in range(SUBLANES):
            p[r * SUBLANES + s, s * SEG + r] = 1.0
    return p


def _s5_operands(lam_re, lam_im, log_dt, b_re, b_im, c_re, c_im):
    lr = jnp.minimum(lam_re.astype(F32), -1e-4)
    li = lam_im.astype(F32)
    dt = jnp.exp(log_dt.astype(F32))[:, None]
    mag = jnp.exp(dt * lr)
    a_re = mag * jnp.cos(dt * li)
    a_im = mag * jnp.sin(dt * li)
    den = lr * lr + li * li
    f_re = ((a_re - 1.0) * lr + a_im * li) / den
    f_im = (a_im * lr - (a_re - 1.0) * li) / den
    br = b_re.astype(F32)
    bi = b_im.astype(F32)
    bb_re = f_re[..., None] * br - f_im[..., None] * bi
    bb_im = f_re[..., None] * bi + f_im[..., None] * br

    n_lb = N_STATE // LW_S5
    gpb = N_SSM_GROUPS // n_lb
    eye = jnp.eye(gpb, dtype=F32)

    def in_blocks(bb):
        x = bb.reshape(n_lb, gpb, SSM_STATE, SSM_GROUP)
        return jnp.einsum("kgph,gq->kghqp", x, eye).reshape(n_lb, gpb * SSM_GROUP, LW_S5)

    def out_blocks(cc):
        x = cc.astype(F32).reshape(n_lb, gpb, SSM_GROUP, SSM_STATE)
        return jnp.einsum("jghp,gq->jgpqh", x, eye).reshape(n_lb, LW_S5, gpb * SSM_GROUP)

    b_mat = jnp.concatenate([in_blocks(bb_re), in_blocks(bb_im)], axis=-1).astype(BF16)
    c_mat = jnp.stack([out_blocks(c_re), -out_blocks(c_im)], axis=1).astype(BF16)

    def power(n):
        m = jnp.exp(n * (dt * lr))
        return (m * jnp.cos(n * (dt * li))).reshape(1, N_STATE), (m * jnp.sin(n * (dt * li))).reshape(1, N_STATE)

    def tiled(v, zero_rows=0):
        t = jnp.broadcast_to(v, (SUBLANES, N_STATE))
        keep = (np.arange(SUBLANES) >= zero_rows)[:, None]
        return jnp.where(keep, t, 0.0)

    a_tab = jnp.stack([tiled(a_re.reshape(1, N_STATE)), tiled(a_im.reshape(1, N_STATE))])
    g_rows = []
    for shift in (1, 2, 4):
        pr, pi = power(float(shift * SEG))
        g_rows += [tiled(pr, shift), tiled(pi, shift)]
    pr, pi = power(float(SEG))
    g_rows += [tiled(pr), tiled(pi)]
    return b_mat, c_mat, a_tab, jnp.stack(g_rows)


def _s5(u, ops, d_skip, w_glu, b_glu, nw, bsz, seq):
    b_mat, c_mat, a_tab, g_tab = ops
    t = T_S5
    tiles = seq // t
    perm = _segment_perm()
    return pl.pallas_call(
        _s5_kernel,
        out_shape=jax.ShapeDtypeStruct((bsz * seq, D_SSM), BF16),
        grid=(bsz, tiles),
        in_specs=[
            pl.BlockSpec((t, D_SSM), lambda b, i: (b * tiles + i, 0)),
            _const_spec((t, t)),
            _const_spec((t, t)),
            _const_spec(b_mat.shape),
            _const_spec(c_mat.shape),
            _const_spec((1, D_SSM)),
            _const_spec(a_tab.shape),
            _const_spec(g_tab.shape),
            _const_spec((D_SSM, D_SSM)),
            _const_spec((1, D_SSM)),
            _const_spec((1, D_SSM)),
        ],
        out_specs=pl.BlockSpec((t, D_SSM), lambda b, i: (b * tiles + i, 0)),
        scratch_shapes=[
            pltpu.VMEM((t, N_STATE), F32),
            pltpu.VMEM((t, N_STATE), F32),
            pltpu.VMEM((t, N_STATE), BF16),
            pltpu.VMEM((t, N_STATE), BF16),
            pltpu.VMEM((2, N_STATE), F32),
        ],
        compiler_params=_params(("arbitrary", "arbitrary")),
        name="s5_scan",
    )(u, jnp.asarray(perm, dtype=BF16), jnp.asarray(perm.T, dtype=BF16), b_mat, c_mat,
      d_skip.reshape(1, D_SSM).astype(F32), a_tab, g_tab, w_glu.astype(BF16),
      b_glu.reshape(1, D_SSM).astype(F32), nw.reshape(1, D_SSM).astype(F32))


def _attn_kernel(sink_ref, q_ref, kc_ref, kp_ref, vc_ref, vp_ref, bias_ref, nw_ref, o_ref):
    first = pl.program_id(1) == 0
    lane = lax.broadcasted_iota(jnp.int32, (BLOCK, 2 * HEAD_DIM), 1)
    low = lane < HEAD_DIM
    col = lax.broadcasted_iota(jnp.int32, (Q_PER_KV * BLOCK, 2 * BLOCK), 1)
    start_mask = jnp.where(jnp.logical_and(first, col < BLOCK), NEG_INF, 0.0)
    ones = jnp.ones((2 * BLOCK, 2 * HEAD_DIM), BF16)
    zero = jnp.zeros((BLOCK, 2 * HEAD_DIM), BF16)
    for jb in range(TQ_ATTN // BLOCK):
        qb = q_ref[jb * BLOCK:(jb + 1) * BLOCK, :]
        pairs = []
        for g in range(N_KV_HEADS):
            kv_lanes = slice(g * 2 * HEAD_DIM, (g + 1) * 2 * HEAD_DIM)
            if jb == 0:
                kb = jnp.concatenate([kp_ref[:, kv_lanes], kc_ref[0:BLOCK, kv_lanes]], axis=0)
                vb = jnp.concatenate([vp_ref[:, kv_lanes], vc_ref[0:BLOCK, kv_lanes]], axis=0)
            else:
                kb = kc_ref[(jb - 1) * BLOCK:(jb + 1) * BLOCK, kv_lanes]
                vb = vc_ref[(jb - 1) * BLOCK:(jb + 1) * BLOCK, kv_lanes]
            stack = []
            for m in range(Q_PER_KV // 2):
                qp = qb[:, (g * 2 + m) * 2 * HEAD_DIM:(g * 2 + m + 1) * 2 * HEAD_DIM]
                stack += [jnp.where(low, qp, zero), jnp.where(low, zero, qp)]
            q4 = jnp.concatenate(stack, axis=0)
            lg = lax.dot_general(q4, kb, (((1,), (1,)), ((), ())), preferred_element_type=F32)
            lg = lg + bias_ref[g * Q_PER_KV:(g + 1) * Q_PER_KV].reshape(Q_PER_KV * BLOCK, 2 * BLOCK)
            if jb == 0:
                lg = lg + start_mask
            sink = jnp.concatenate(
                [jnp.full((BLOCK, 1), sink_ref[g * Q_PER_KV + e], F32) for e in range(Q_PER_KV)], axis=0)
            mx = jnp.maximum(jnp.max(lg, axis=-1, keepdims=True), sink)
            p = jnp.exp(lg - mx).astype(BF16)
            pv = jnp.dot(p, jnp.concatenate([vb, ones], axis=1), preferred_element_type=F32)
            o4 = pv[:, :2 * HEAD_DIM] / (pv[:, 2 * HEAD_DIM:] + jnp.exp(sink - mx))
            for m in range(Q_PER_KV // 2):
                pairs.append(jnp.where(low, o4[(2 * m) * BLOCK:(2 * m + 1) * BLOCK],
                                       o4[(2 * m + 1) * BLOCK:(2 * m + 2) * BLOCK]))
        y = jnp.concatenate(pairs, axis=1)
        ms = jnp.mean(y * y, axis=-1, keepdims=True)
        o_ref[jb * BLOCK:(jb + 1) * BLOCK, :] = (y * lax.rsqrt(ms + EPS) * nw_ref[...]).astype(BF16)


def _attention(q, kd, vd, bias, sinks, nw, bsz, seq):
    tq = TQ_ATTN
    tiles = seq // tq
    ratio = tq // BLOCK

    def prev_map(b, i):
        return (jnp.maximum((b * tiles + i) * ratio - 1, 0), 0)

    return pl.pallas_call(
        _attn_kernel,
        out_shape=jax.ShapeDtypeStruct((bsz * seq, D_ATTN), BF16),
        grid=(bsz, tiles),
        in_specs=[
            pl.BlockSpec(memory_space=pltpu.SMEM),
            pl.BlockSpec((tq, D_ATTN), lambda b, i: (b * tiles + i, 0)),
            pl.BlockSpec((tq, 2 * KV_DIM), lambda b, i: (b * tiles + i, 0)),
            pl.BlockSpec((BLOCK, 2 * KV_DIM), prev_map),
            pl.BlockSpec((tq, 2 * KV_DIM), lambda b, i: (b * tiles + i, 0)),
            pl.BlockSpec((BLOCK, 2 * KV_DIM), prev_map),
            _const_spec((N_Q_HEADS, BLOCK, 2 * BLOCK)),
            _const_spec((1, D_ATTN)),
        ],
        out_specs=pl.BlockSpec((tq, D_ATTN), lambda b, i: (b * tiles + i, 0)),
        compiler_params=_params(("arbitrary", "arbitrary")),
        name="swa_attention",
    )(sinks.astype(F32), q, kd, kd, vd, vd, bias, nw.reshape(1, D_ATTN).astype(F32))


def _ffn_kernel(x_ref, ys_ref, ya_ref, mod_ref, wos_ref, woa_ref, n2w_ref, wv_ref, wg_ref,
                cwv_ref, cwg_ref, cbv_ref, cbg_ref, wd_ref, o_ref, h_ref, acc_ref, tail_ref,
                upv_ref, upg_ref, *, tiles_per_batch):
    first = (pl.program_id(0) % tiles_per_batch) == 0
    g1 = mod_ref[2:3, :]
    sh2 = mod_ref[3:4, :]
    sc2 = mod_ref[4:5, :]
    g2 = mod_ref[5:6, :]
    mixed = jnp.dot(ys_ref[...], wos_ref[...], preferred_element_type=F32)
    mixed = mixed + jnp.dot(ya_ref[...], woa_ref[...], preferred_element_type=F32)
    x1 = x_ref[...] + g1 * mixed
    ms = jnp.mean(x1 * x1, axis=-1, keepdims=True)
    h_ref[...] = ((x1 * lax.rsqrt(ms + EPS) * n2w_ref[...]) * (1.0 + sc2) + sh2).astype(BF16)
    acc_ref[...] = jnp.zeros_like(acc_ref)
    tm = x_ref.shape[0]

    def conv(up, cw, cb, tail_slot):
        prev = jnp.where(first, 0.0, tail_ref[tail_slot])
        tail_ref[tail_slot] = up[tm - SUBLANES:, :]
        ext = jnp.concatenate([prev, up], axis=0)
        up1 = pltpu.roll(ext, 1, axis=0)[SUBLANES:, :]
        up2 = pltpu.roll(ext, 2, axis=0)[SUBLANES:, :]
        return cw[0:1, :] * up2 + cw[1:2, :] * up1 + cw[2:3, :] * up + cb

    def up_proj(j, slot):
        h = h_ref[...]
        upv_ref[slot] = jnp.dot(h, wv_ref[j], preferred_element_type=F32)
        upg_ref[slot] = jnp.dot(h, wg_ref[j], preferred_element_type=F32)

    def consume(j, slot):
        val = conv(upv_ref[slot], cwv_ref[j], cbv_ref[j], 2 * j)
        gate = conv(upg_ref[slot], cwg_ref[j], cbg_ref[j], 2 * j + 1)
        act = (gate * (1.0 / (1.0 + jnp.exp(-gate))) * val).astype(BF16)
        acc_ref[...] += jnp.dot(act, wd_ref[j], preferred_element_type=F32)

    up_proj(0, 0)

    def two_chunks(jj, carry):
        j = 2 * jj
        up_proj(j + 1, 1)
        consume(j, 0)
        up_proj(j + 2, 0)
        consume(j + 1, 1)
        return carry

    lax.fori_loop(0, (N_FF_CHUNKS - 1) // 2, two_chunks, 0)
    consume(N_FF_CHUNKS - 1, 0)
    o_ref[...] = x1 + g2 * acc_ref[...]


def _out_ffn(x2d, ys, ya, mod_l, w_out, n2w, w_up, conv_w, conv_b, w_down, tiles_per_batch):
    ntok = x2d.shape[0]
    tm = TM_FFN
    nc, tf = N_FF_CHUNKS, TF_FFN

    def cols(a):
        r = a.shape[0]
        v = a[:, :D_FF].reshape(r, nc, tf).transpose(1, 0, 2)
        g = a[:, D_FF:].reshape(r, nc, tf).transpose(1, 0, 2)
        return v, g

    wv, wg = cols(w_up.astype(BF16))
    cwv, cwg = cols(conv_w.astype(F32))
    cbv, cbg = cols(conv_b.reshape(1, 2 * D_FF).astype(F32))
    wd = w_down.astype(BF16).reshape(nc, tf, D_MODEL)
    wo = w_out.astype(BF16)
    return pl.pallas_call(
        functools.partial(_ffn_kernel, tiles_per_batch=tiles_per_batch),
        out_shape=jax.ShapeDtypeStruct((ntok, D_MODEL), F32),
        grid=(ntok // tm,),
        in_specs=[
            pl.BlockSpec((tm, D_MODEL), lambda i: (i, 0)),
            pl.BlockSpec((tm, D_SSM), lambda i: (i, 0)),
            pl.BlockSpec((tm, D_ATTN), lambda i: (i, 0)),
            pl.BlockSpec((None, N_MOD, D_MODEL), lambda i: (i // tiles_per_batch, 0, 0)),
            _const_spec((D_SSM, D_MODEL)),
            _const_spec((D_ATTN, D_MODEL)),
            _const_spec((1, D_MODEL)),
            _const_spec((nc, D_MODEL, tf)),
            _const_spec((nc, D_MODEL, tf)),
            _const_spec((nc, 3, tf)),
            _const_spec((nc, 3, tf)),
            _const_spec((nc, 1, tf)),
            _const_spec((nc, 1, tf)),
            _const_spec((nc, tf, D_MODEL)),
        ],
        out_specs=pl.BlockSpec((tm, D_MODEL), lambda i: (i, 0)),
        scratch_shapes=[
            pltpu.VMEM((tm, D_MODEL), BF16),
            pltpu.VMEM((tm, D_MODEL), F32),
            pltpu.VMEM((2 * nc, SUBLANES, tf), F32),
            pltpu.VMEM((2, tm, tf), F32),
            pltpu.VMEM((2, tm, tf), F32),
        ],
        compiler_params=_params(("arbitrary",)),
        name="out_proj_convffn",
    )(x2d, ys, ya, mod_l, wo[:D_SSM], wo[D_SSM:], n2w.reshape(1, D_MODEL).astype(F32),
      wv, wg, cwv, cwg, cbv, cbg, wd)


def _in_weights(w_in_l, qw, kw):
    wu = w_in_l[:, :D_SSM]
    wq = w_in_l[:, D_SSM:D_SSM + D_ATTN]
    wk = w_in_l[:, D_SSM + D_ATTN:D_SSM + D_ATTN + KV_DIM].reshape(D_MODEL, N_KV_HEADS, 1, HEAD_DIM)
    wv = w_in_l[:, D_SSM + D_ATTN + KV_DIM:].reshape(D_MODEL, N_KV_HEADS, 1, HEAD_DIM)
    wk2 = jnp.broadcast_to(wk, (D_MODEL, N_KV_HEADS, 2, HEAD_DIM)).reshape(D_MODEL, 2 * KV_DIM)
    wv2 = jnp.broadcast_to(wv, (D_MODEL, N_KV_HEADS, 2, HEAD_DIM)).reshape(D_MODEL, 2 * KV_DIM)
    w_ext = jnp.concatenate([wu, wq, wk2, wv2], axis=1).astype(BF16)
    qw_ext = (jnp.tile(qw.astype(F32), N_Q_HEADS) * (HEAD_DIM ** -0.5)).reshape(1, D_ATTN)
    kw_ext = jnp.tile(kw.astype(F32), 2 * N_KV_HEADS).reshape(1, 2 * KV_DIM)
    return w_ext, qw_ext, kw_ext


def kernel(x, c, w_mod, b_mod, norm1_w, w_in, lam_re, lam_im, log_dt, ssm_b_re, ssm_b_im, ssm_c_re,
           ssm_c_im, ssm_d, w_glu, b_glu, q_norm_w, k_norm_w, rel_bias, sinks, out_norm_ssm,
           out_norm_attn, w_out, norm2_w, w_up, conv_w, conv_b, w_down):
    bsz, seq, _ = x.shape
    assert seq % T_S5 == 0 and seq % TQ_ATTN == 0 and seq % TM_IN == 0 and seq % TM_FFN == 0
    mod = _modulation(c.astype(F32), w_mod.astype(F32), b_mod.astype(F32))
    mod = mod.reshape(DEPTH, bsz, N_MOD, D_MODEL)
    bias = _bias_table(rel_bias.astype(F32))
    x2d = x.reshape(bsz * seq, D_MODEL).astype(F32)
    for l in range(DEPTH):
        w_ext, qw_ext, kw_ext = _in_weights(w_in[l], q_norm_w[l], k_norm_w[l])
        u, q, kd, vd = _in_projection(x2d, mod[l], norm1_w[l].reshape(1, D_MODEL).astype(F32),
                                      w_ext, qw_ext, kw_ext, seq // TM_IN)
        ops = _s5_operands(lam_re[l], lam_im[l], log_dt[l], ssm_b_re[l], ssm_b_im[l],
                           ssm_c_re[l], ssm_c_im[l])
        ys = _s5(u, ops, ssm_d[l], w_glu[l], b_glu[l], out_norm_ssm[l], bsz, seq)
        ya = _attention(q, kd, vd, bias, sinks[l], out_norm_attn[l], bsz, seq)
        x2d = _out_ffn(x2d, ys, ya, mod[l], w_out[l], norm2_w[l], w_up[l], conv_w[l], conv_b[l],
                       w_down[l], seq // TM_FFN)
    return x2d.reshape(bsz, seq, D_MODEL).astype(x.dtype)
```

```python
import functools
import math

import numpy as np
import jax
import jax.numpy as jnp
from jax import lax
from jax.experimental import pallas as pl
from jax.experimental.pallas import tpu as pltpu

D_MODEL = 1024
DEPTH = 2
D_SSM = 512
SSM_GROUP = 16
N_SSM_GROUPS = D_SSM // SSM_GROUP
SSM_STATE = 64
N_STATE = N_SSM_GROUPS * SSM_STATE
D_ATTN = D_MODEL - D_SSM
HEAD_DIM = 64
N_Q_HEADS = D_ATTN // HEAD_DIM
N_KV_HEADS = 2
Q_PER_KV = N_Q_HEADS // N_KV_HEADS
KV_DIM = N_KV_HEADS * HEAD_DIM
WINDOW = 128
BLOCK = 128
N_BUCKETS = 32
MAX_DISTANCE = 128
D_FF = 2816
N_MOD = 6
EPS = 1e-6
NEG_INF = -1e30

SUBLANES = 8
LANES = 128
VMEM_LIMIT_BYTES = 56 * 1024 * 1024

TM_IN = 512
T_S5 = 512
SEG = T_S5 // SUBLANES
LW_S5 = 512
TQ_ATTN = 512
TM_FFN = 512
TF_FFN = 256
N_FF_CHUNKS = D_FF // TF_FFN
D_PROJ = D_SSM + D_ATTN + 4 * KV_DIM

F32 = jnp.float32
BF16 = jnp.bfloat16


def _params(semantics):
    return pltpu.CompilerParams(dimension_semantics=semantics, vmem_limit_bytes=VMEM_LIMIT_BYTES)


def _const_spec(shape):
    nd = len(shape)
    return pl.BlockSpec(shape, lambda *_: (0,) * nd, pipeline_mode=pl.Buffered(1))


def _mod_kernel(c_ref, w_ref, b_ref, o_ref):
    c = c_ref[...]
    c_act = c * (1.0 / (1.0 + jnp.exp(-c)))
    o_ref[...] = jnp.dot(c_act, w_ref[...], preferred_element_type=F32) + b_ref[...]


def _modulation(c, w_mod, b_mod):
    bsz = c.shape[0]
    tn = 1536
    n = N_MOD * D_MODEL
    return pl.pallas_call(
        _mod_kernel,
        out_shape=jax.ShapeDtypeStruct((DEPTH, bsz, n), F32),
        grid=(DEPTH, n // tn),
        in_specs=[
            pl.BlockSpec((bsz, D_MODEL), lambda l, j: (0, 0)),
            pl.BlockSpec((None, D_MODEL, tn), lambda l, j: (l, 0, j)),
            pl.BlockSpec((None, 1, tn), lambda l, j: (l, 0, j)),
        ],
        out_specs=pl.BlockSpec((None, bsz, tn), lambda l, j: (l, 0, j)),
        compiler_params=_params(("arbitrary", "arbitrary")),
        name="adaln_mod",
    )(c, w_mod, b_mod.reshape(DEPTH, 1, n))


def _t5_bucket(n):
    n = np.maximum(n, 0)
    max_exact = N_BUCKETS // 2
    log_part = np.log(np.maximum(n, 1) / max_exact) / math.log(MAX_DISTANCE / max_exact)
    large = max_exact + (log_part * (N_BUCKETS - max_exact)).astype(np.int32)
    large = np.minimum(large, N_BUCKETS - 1)
    return np.where(n < max_exact, n, large).astype(np.int32)


def _band_buckets():
    dist = (np.arange(BLOCK)[:, None] + BLOCK) - np.arange(2 * BLOCK)[None, :]
    valid = (dist >= 0) & (dist < WINDOW)
    return np.where(valid, _t5_bucket(dist), -1).astype(np.int32)


def _bias_kernel(rb_ref, bucket_ref, o_ref):
    h = pl.program_id(0)
    bucket = bucket_ref[...]
    acc = jnp.full(bucket.shape, NEG_INF, F32)
    for b in range(N_BUCKETS):
        acc = jnp.where(bucket == b, rb_ref[b, h], acc)
    o_ref[...] = acc


def _bias_table(rel_bias):
    return pl.pallas_call(
        _bias_kernel,
        out_shape=jax.ShapeDtypeStruct((N_Q_HEADS, BLOCK, 2 * BLOCK), F32),
        grid=(N_Q_HEADS,),
        in_specs=[
            pl.BlockSpec(memory_space=pltpu.SMEM),
            pl.BlockSpec((BLOCK, 2 * BLOCK), lambda h: (0, 0)),
        ],
        out_specs=pl.BlockSpec((None, BLOCK, 2 * BLOCK), lambda h: (h, 0, 0)),
        compiler_params=_params(("arbitrary",)),
        name="rel_bias_table",
    )(rel_bias, jnp.asarray(_band_buckets()))


def _in_kernel(x_ref, mod_ref, n1w_ref, w_ref, segq_ref, segk_ref, qw_ref, kw_ref,
               u_ref, q_ref, k_ref, v_ref):
    x = x_ref[...]
    sh1 = mod_ref[0:1, :]
    sc1 = mod_ref[1:2, :]
    ms = jnp.mean(x * x, axis=-1, keepdims=True)
    h = (x * lax.rsqrt(ms + EPS) * n1w_ref[...]) * (1.0 + sc1) + sh1
    proj = jnp.dot(h.astype(BF16), w_ref[...], preferred_element_type=F32)
    u_ref[...] = proj[:, :D_SSM].astype(BF16)
    q = proj[:, D_SSM:D_SSM + D_ATTN]
    k = proj[:, D_SSM + D_ATTN:D_SSM + D_ATTN + 2 * KV_DIM]
    v = proj[:, D_SSM + D_ATTN + 2 * KV_DIM:]
    q_ms = jnp.dot((q * q).astype(BF16), segq_ref[...], preferred_element_type=F32) * (1.0 / HEAD_DIM)
    k_ms = jnp.dot((k * k).astype(BF16), segk_ref[...], preferred_element_type=F32) * (1.0 / HEAD_DIM)
    q_ref[...] = (q * lax.rsqrt(q_ms + EPS) * qw_ref[...]).astype(BF16)
    k_ref[...] = (k * lax.rsqrt(k_ms + EPS) * kw_ref[...]).astype(BF16)
    v_ref[...] = v.astype(BF16)


def _seg_ones(n):
    idx = np.arange(n) // HEAD_DIM
    return jnp.asarray((idx[:, None] == idx[None, :]).astype(np.float32), dtype=BF16)


def _in_projection(x2d, mod_l, n1w, w_ext, qw_ext, kw_ext, tiles_per_batch):
    ntok = x2d.shape[0]
    tm = TM_IN
    return pl.pallas_call(
        _in_kernel,
        out_shape=(
            jax.ShapeDtypeStruct((ntok, D_SSM), BF16),
            jax.ShapeDtypeStruct((ntok, D_ATTN), BF16),
            jax.ShapeDtypeStruct((ntok, 2 * KV_DIM), BF16),
            jax.ShapeDtypeStruct((ntok, 2 * KV_DIM), BF16),
        ),
        grid=(ntok // tm,),
        in_specs=[
            pl.BlockSpec((tm, D_MODEL), lambda i: (i, 0)),
            pl.BlockSpec((None, N_MOD, D_MODEL), lambda i: (i // tiles_per_batch, 0, 0)),
            _const_spec((1, D_MODEL)),
            _const_spec((D_MODEL, D_PROJ)),
            _const_spec((D_ATTN, D_ATTN)),
            _const_spec((2 * KV_DIM, 2 * KV_DIM)),
            _const_spec((1, D_ATTN)),
            _const_spec((1, 2 * KV_DIM)),
        ],
        out_specs=(
            pl.BlockSpec((tm, D_SSM), lambda i: (i, 0)),
            pl.BlockSpec((tm, D_ATTN), lambda i: (i, 0)),
            pl.BlockSpec((tm, 2 * KV_DIM), lambda i: (i, 0)),
            pl.BlockSpec((tm, 2 * KV_DIM), lambda i: (i, 0)),
        ),
        compiler_params=_params(("arbitrary",)),
        name="in_projection",
    )(x2d, mod_l, n1w, w_ext, _seg_ones(D_ATTN), _seg_ones(2 * KV_DIM), qw_ext, kw_ext)


def _cmul(ar, ai, br, bi):
    return ar * br - ai * bi, ar * bi + ai * br


def _s5_kernel(u_ref, perm_ref, permt_ref, b_ref, c_ref, d_ref, a_ref, g_ref,
               wglu_ref, bglu_ref, nw_ref, o_ref, sre_ref, sim_ref, sbre_ref, sbim_ref, carry_ref):
    @pl.when(pl.program_id(1) == 0)
    def _():
        carry_ref[...] = jnp.zeros_like(carry_ref)

    up = jnp.dot(perm_ref[...], u_ref[...], preferred_element_type=F32).astype(BF16)
    n_lb = N_STATE // LW_S5
    cw = D_SSM // n_lb
    row = lax.broadcasted_iota(jnp.int32, (SUBLANES, LW_S5), 0)

    def in_matmul(lb):
        lanes = slice(lb * LW_S5, (lb + 1) * LW_S5)
        bu = jnp.dot(up[:, lb * cw:(lb + 1) * cw], b_ref[lb], preferred_element_type=F32)
        sre_ref[:, lanes] = bu[:, :LW_S5]
        sim_ref[:, lanes] = bu[:, LW_S5:]

    def scan(lb):
        lanes = slice(lb * LW_S5, (lb + 1) * LW_S5)
        ar = a_ref[0, :, lanes]
        ai = a_ref[1, :, lanes]
        er = jnp.zeros((SUBLANES, LW_S5), F32)
        ei = er
        for r in range(SEG):
            rows = slice(r * SUBLANES, (r + 1) * SUBLANES)
            pr, pi = _cmul(ar, ai, er, ei)
            er = pr + sre_ref[rows, lanes]
            ei = pi + sim_ref[rows, lanes]
            sre_ref[rows, lanes] = er
            sim_ref[rows, lanes] = ei
        cin_r = jnp.broadcast_to(carry_ref[0:1, lanes], (SUBLANES, LW_S5))
        cin_i = jnp.broadcast_to(carry_ref[1:2, lanes], (SUBLANES, LW_S5))
        cr = jnp.where(row == 0, cin_r, pltpu.roll(er, 1, axis=0))
        ci = jnp.where(row == 0, cin_i, pltpu.roll(ei, 1, axis=0))
        for step, shift in enumerate((1, 2, 4)):
            gr = g_ref[2 * step, :, lanes]
            gi = g_ref[2 * step + 1, :, lanes]
            tr, ti = _cmul(gr, gi, pltpu.roll(cr, shift, axis=0), pltpu.roll(ci, shift, axis=0))
            cr = cr + tr
            ci = ci + ti
        nxt_r, nxt_i = _cmul(g_ref[6, :, lanes], g_ref[7, :, lanes], cr, ci)
        carry_ref[0:1, lanes] = (nxt_r + er)[SUBLANES - 1:SUBLANES, :]
        carry_ref[1:2, lanes] = (nxt_i + ei)[SUBLANES - 1:SUBLANES, :]
        qr, qi = cr, ci
        for r2 in range(SEG // 2):
            vr, vi = [], []
            for r in (2 * r2, 2 * r2 + 1):
                rows = slice(r * SUBLANES, (r + 1) * SUBLANES)
                qr, qi = _cmul(ar, ai, qr, qi)
                vr.append(sre_ref[rows, lanes] + qr)
                vi.append(sim_ref[rows, lanes] + qi)
            rows2 = slice(r2 * 2 * SUBLANES, (r2 + 1) * 2 * SUBLANES)
            sbre_ref[rows2, lanes] = jnp.concatenate(vr, axis=0).astype(BF16)
            sbim_ref[rows2, lanes] = jnp.concatenate(vi, axis=0).astype(BF16)

    def out_matmul(lb):
        lanes = slice(lb * LW_S5, (lb + 1) * LW_S5)
        y = jnp.dot(sbre_ref[:, lanes], c_ref[lb, 0], preferred_element_type=F32)
        return y + jnp.dot(sbim_ref[:, lanes], c_ref[lb, 1], preferred_element_type=F32)

    ys = []
    in_matmul(0)
    for lb in range(n_lb):
        if lb + 1 < n_lb:
            in_matmul(lb + 1)
        scan(lb)
        ys.append(out_matmul(lb))
    y = jnp.concatenate(ys, axis=1) + d_ref[...] * up.astype(F32)
    z = jax.nn.gelu(y)
    gate = jnp.dot(z.astype(BF16), wglu_ref[...], preferred_element_type=F32) + bglu_ref[...]
    out = z * (1.0 / (1.0 + jnp.exp(-gate)))
    ms = jnp.mean(out * out, axis=-1, keepdims=True)
    outn = (out * lax.rsqrt(ms + EPS) * nw_ref[...]).astype(BF16)
    o_ref[...] = jnp.dot(permt_ref[...], outn, preferred_element_type=F32).astype(BF16)


def _segment_perm():
    p = np.zeros((T_S5, T_S5), np.float32)
    for r in range(SEG):
        for s in range(SUBLANES):
            p[r * SUBLANES + s, s * SEG + r] = 1.0
    return p


def _s5_operands(lam_re, lam_im, log_dt, b_re, b_im, c_re, c_im):
    lr = jnp.minimum(lam_re.astype(F32), -1e-4)
    li = lam_im.astype(F32)
    dt = jnp.exp(log_dt.astype(F32))[:, None]
    mag = jnp.exp(dt * lr)
    a_re = mag * jnp.cos(dt * li)
    a_im = mag * jnp.sin(dt * li)
    den = lr * lr + li * li
    f_re = ((a_re - 1.0) * lr + a_im * li) / den
    f_im = (a_im * lr - (a_re - 1.0) * li) / den
    br = b_re.astype(F32)
    bi = b_im.astype(F32)
    bb_re = f_re[..., None] * br - f_im[..., None] * bi
    bb_im = f_re[..., None] * bi + f_im[..., None] * br

    n_lb = N_STATE // LW_S5
    gpb = N_SSM_GROUPS // n_lb
    eye = jnp.eye(gpb, dtype=F32)

    def in_blocks(bb):
        x = bb.reshape(n_lb, gpb, SSM_STATE, SSM_GROUP)
        return jnp.einsum("kgph,gq->kghqp", x, eye).reshape(n_lb, gpb * SSM_GROUP, LW_S5)

    def out_blocks(cc):
        x = cc.astype(F32).reshape(n_lb, gpb, SSM_GROUP, SSM_STATE)
        return jnp.einsum("jghp,gq->jgpqh", x, eye).reshape(n_lb, LW_S5, gpb * SSM_GROUP)

    b_mat = jnp.concatenate([in_blocks(bb_re), in_blocks(bb_im)], axis=-1).astype(BF16)
    c_mat = jnp.stack([out_blocks(c_re), -out_blocks(c_im)], axis=1).astype(BF16)

    def power(n):
        m = jnp.exp(n * (dt * lr))
        return (m * jnp.cos(n * (dt * li))).reshape(1, N_STATE), (m * jnp.sin(n * (dt * li))).reshape(1, N_STATE)

    def tiled(v, zero_rows=0):
        t = jnp.broadcast_to(v, (SUBLANES, N_STATE))
        keep = (np.arange(SUBLANES) >= zero_rows)[:, None]
        return jnp.where(keep, t, 0.0)

    a_tab = jnp.stack([tiled(a_re.reshape(1, N_STATE)), tiled(a_im.reshape(1, N_STATE))])
    g_rows = []
    for shift in (1, 2, 4):
        pr, pi = power(float(shift * SEG))
        g_rows += [tiled(pr, shift), tiled(pi, shift)]
    pr, pi = power(float(SEG))
    g_rows += [tiled(pr), tiled(pi)]
    return b_mat, c_mat, a_tab, jnp.stack(g_rows)


def _s5(u, ops, d_skip, w_glu, b_glu, nw, bsz, seq):
    b_mat, c_mat, a_tab, g_tab = ops
    t = T_S5
    tiles = seq // t
    perm = _segment_perm()
    return pl.pallas_call(
        _s5_kernel,
        out_shape=jax.ShapeDtypeStruct((bsz * seq, D_SSM), BF16),
        grid=(bsz, tiles),
        in_specs=[
            pl.BlockSpec((t, D_SSM), lambda b, i: (b * tiles + i, 0)),
            _const_spec((t, t)),
            _const_spec((t, t)),
            _const_spec(b_mat.shape),
            _const_spec(c_mat.shape),
            _const_spec((1, D_SSM)),
            _const_spec(a_tab.shape),
            _const_spec(g_tab.shape),
            _const_spec((D_SSM, D_SSM)),
            _const_spec((1, D_SSM)),
            _const_spec((1, D_SSM)),
        ],
        out_specs=pl.BlockSpec((t, D_SSM), lambda b, i: (b * tiles + i, 0)),
        scratch_shapes=[
            pltpu.VMEM((t, N_STATE), F32),
            pltpu.VMEM((t, N_STATE), F32),
            pltpu.VMEM((t, N_STATE), BF16),
            pltpu.VMEM((t, N_STATE), BF16),
            pltpu.VMEM((2, N_STATE), F32),
        ],
        compiler_params=_params(("arbitrary", "arbitrary")),
        name="s5_scan",
    )(u, jnp.asarray(perm, dtype=BF16), jnp.asarray(perm.T, dtype=BF16), b_mat, c_mat,
      d_skip.reshape(1, D_SSM).astype(F32), a_tab, g_tab, w_glu.astype(BF16),
      b_glu.reshape(1, D_SSM).astype(F32), nw.reshape(1, D_SSM).astype(F32))


def _attn_kernel(sink_ref, q_ref, kc_ref, kp_ref, vc_ref, vp_ref, bias_ref, nw_ref, o_ref,
                 lg_ref, p_ref, es_ref, y_ref):
    first = pl.program_id(1) == 0
    pw = 2 * HEAD_DIM
    lane = lax.broadcasted_iota(jnp.int32, (BLOCK, pw), 1)
    low = lane < HEAD_DIM
    col = lax.broadcasted_iota(jnp.int32, (Q_PER_KV * BLOCK, 2 * BLOCK), 1)
    start_mask = jnp.where(jnp.logical_and(first, col < BLOCK), NEG_INF, 0.0)
    ones = jnp.ones((2 * BLOCK, pw), BF16)
    zero = jnp.zeros((BLOCK, pw), BF16)
    n_items = (TQ_ATTN // BLOCK) * N_KV_HEADS

    def band(cur_ref, prev_ref, jb, g):
        kv_lanes = slice(g * pw, (g + 1) * pw)
        if jb == 0:
            return jnp.concatenate([prev_ref[:, kv_lanes], cur_ref[0:BLOCK, kv_lanes]], axis=0)
        return cur_ref[(jb - 1) * BLOCK:(jb + 1) * BLOCK, kv_lanes]

    def logits(it):
        jb, g = divmod(it, N_KV_HEADS)
        qb = q_ref[jb * BLOCK:(jb + 1) * BLOCK, :]
        stack = []
        for m in range(Q_PER_KV // 2):
            qp = qb[:, (g * 2 + m) * pw:(g * 2 + m + 1) * pw]
            stack += [jnp.where(low, qp, zero), jnp.where(low, zero, qp)]
        q4 = jnp.concatenate(stack, axis=0)
        lg = lax.dot_general(q4, band(kc_ref, kp_ref, jb, g), (((1,), (1,)), ((), ())),
                             preferred_element_type=F32)
        lg = lg + bias_ref[g * Q_PER_KV:(g + 1) * Q_PER_KV].reshape(Q_PER_KV * BLOCK, 2 * BLOCK)
        if jb == 0:
            lg = lg + start_mask
        lg_ref[it % 2] = lg

    def softmax(it):
        g = it % N_KV_HEADS
        lg = lg_ref[it % 2]
        sink = jnp.concatenate(
            [jnp.full((BLOCK, pw), sink_ref[g * Q_PER_KV + e], F32) for e in range(Q_PER_KV)], axis=0)
        mx = jnp.maximum(jnp.broadcast_to(jnp.max(lg, axis=-1, keepdims=True), sink.shape), sink)
        p_ref[it % 2, :, :pw] = jnp.exp(lg[:, :pw] - mx).astype(BF16)
        p_ref[it % 2, :, pw:] = jnp.exp(lg[:, pw:] - mx).astype(BF16)
        es_ref[it % 2] = jnp.exp(sink - mx)

    def weighted_sum(it):
        jb, g = divmod(it, N_KV_HEADS)
        vext = jnp.concatenate([band(vc_ref, vp_ref, jb, g), ones], axis=1)
        pv = jnp.dot(p_ref[it % 2], vext, preferred_element_type=F32)
        o4 = pv[:, :pw] / (pv[:, pw:] + es_ref[it % 2])
        for m in range(Q_PER_KV // 2):
            y_ref[jb * BLOCK:(jb + 1) * BLOCK, (g * 2 + m) * pw:(g * 2 + m + 1) * pw] = jnp.where(
                low, o4[(2 * m) * BLOCK:(2 * m + 1) * BLOCK], o4[(2 * m + 1) * BLOCK:(2 * m + 2) * BLOCK])
        if g == N_KV_HEADS - 1:
            y = y_ref[jb * BLOCK:(jb + 1) * BLOCK, :]
            ms = jnp.mean(y * y, axis=-1, keepdims=True)
            o_ref[jb * BLOCK:(jb + 1) * BLOCK, :] = (y * lax.rsqrt(ms + EPS) * nw_ref[...]).astype(BF16)

    for t in range(n_items + 2):
        if t < n_items:
            logits(t)
        if 0 <= t - 1 < n_items:
            softmax(t - 1)
        if 0 <= t - 2 < n_items:
            weighted_sum(t - 2)


def _attention(q, kd, vd, bias, sinks, nw, bsz, seq):
    tq = TQ_ATTN
    tiles = seq // tq
    ratio = tq // BLOCK

    def prev_map(b, i):
        return (jnp.maximum((b * tiles + i) * ratio - 1, 0), 0)

    return pl.pallas_call(
        _attn_kernel,
        out_shape=jax.ShapeDtypeStruct((bsz * seq, D_ATTN), BF16),
        grid=(bsz, tiles),
        in_specs=[
            pl.BlockSpec(memory_space=pltpu.SMEM),
            pl.BlockSpec((tq, D_ATTN), lambda b, i: (b * tiles + i, 0)),
            pl.BlockSpec((tq, 2 * KV_DIM), lambda b, i: (b * tiles + i, 0)),
            pl.BlockSpec((BLOCK, 2 * KV_DIM), prev_map),
            pl.BlockSpec((tq, 2 * KV_DIM), lambda b, i: (b * tiles + i, 0)),
            pl.BlockSpec((BLOCK, 2 * KV_DIM), prev_map),
            _const_spec((N_Q_HEADS, BLOCK, 2 * BLOCK)),
            _const_spec((1, D_ATTN)),
        ],
        out_specs=pl.BlockSpec((tq, D_ATTN), lambda b, i: (b * tiles + i, 0)),
        scratch_shapes=[
            pltpu.VMEM((2, Q_PER_KV * BLOCK, 2 * BLOCK), F32),
            pltpu.VMEM((2, Q_PER_KV * BLOCK, 2 * BLOCK), BF16),
            pltpu.VMEM((2, Q_PER_KV * BLOCK, 2 * HEAD_DIM), F32),
            pltpu.VMEM((tq, D_ATTN), F32),
        ],
        compiler_params=_params(("arbitrary", "arbitrary")),
        name="swa_attention",
    )(sinks.astype(F32), q, kd, kd, vd, vd, bias, nw.reshape(1, D_ATTN).astype(F32))


def _ffn_kernel(x_ref, ys_ref, ya_ref, mod_ref, wos_ref, woa_ref, n2w_ref, wv_ref, wg_ref,
                cwv_ref, cwg_ref, cbv_ref, cbg_ref, wd_ref, o_ref, h_ref, acc_ref, tail_ref,
                upv_ref, upg_ref, *, tiles_per_batch):
    first = (pl.program_id(0) % tiles_per_batch) == 0
    g1 = mod_ref[2:3, :]
    sh2 = mod_ref[3:4, :]
    sc2 = mod_ref[4:5, :]
    g2 = mod_ref[5:6, :]
    mixed = jnp.dot(ys_ref[...], wos_ref[...], preferred_element_type=F32)
    mixed = mixed + jnp.dot(ya_ref[...], woa_ref[...], preferred_element_type=F32)
    x1 = x_ref[...] + g1 * mixed
    ms = jnp.mean(x1 * x1, axis=-1, keepdims=True)
    h_ref[...] = ((x1 * lax.rsqrt(ms + EPS) * n2w_ref[...]) * (1.0 + sc2) + sh2).astype(BF16)
    acc_ref[...] = jnp.zeros_like(acc_ref)
    tm = x_ref.shape[0]

    def conv(up, cw, cb, tail_slot):
        prev = jnp.where(first, 0.0, tail_ref[tail_slot])
        tail_ref[tail_slot] = up[tm - SUBLANES:, :]
        ext = jnp.concatenate([prev, up], axis=0)
        up1 = pltpu.roll(ext, 1, axis=0)[SUBLANES:, :]
        up2 = pltpu.roll(ext, 2, axis=0)[SUBLANES:, :]
        return cw[0:1, :] * up2 + cw[1:2, :] * up1 + cw[2:3, :] * up + cb

    def up_proj(j, slot):
        h = h_ref[...]
        upv_ref[slot] = jnp.dot(h, wv_ref[j], preferred_element_type=F32)
        upg_ref[slot] = jnp.dot(h, wg_ref[j], preferred_element_type=F32)

    def consume(j, slot):
        val = conv(upv_ref[slot], cwv_ref[j], cbv_ref[j], 2 * j)
        gate = conv(upg_ref[slot], cwg_ref[j], cbg_ref[j], 2 * j + 1)
        act = (gate * (1.0 / (1.0 + jnp.exp(-gate))) * val).astype(BF16)
        acc_ref[...] += jnp.dot(act, wd_ref[j], preferred_element_type=F32)

    up_proj(0, 0)

    def two_chunks(jj, carry):
        j = 2 * jj
        up_proj(j + 1, 1)
        consume(j, 0)
        up_proj(j + 2, 0)
        consume(j + 1, 1)
        return carry

    lax.fori_loop(0, (N_FF_CHUNKS - 1) // 2, two_chunks, 0)
    consume(N_FF_CHUNKS - 1, 0)
    o_ref[...] = x1 + g2 * acc_ref[...]


def _out_ffn(x2d, ys, ya, mod_l, w_out, n2w, w_up, conv_w, conv_b, w_down, tiles_per_batch):
    ntok = x2d.shape[0]
    tm = TM_FFN
    nc, tf = N_FF_CHUNKS, TF_FFN

    def cols(a):
        r = a.shape[0]
        v = a[:, :D_FF].reshape(r, nc, tf).transpose(1, 0, 2)
        g = a[:, D_FF:].reshape(r, nc, tf).transpose(1, 0, 2)
        return v, g

    wv, wg = cols(w_up.astype(BF16))
    cwv, cwg = cols(conv_w.astype(F32))
    cbv, cbg = cols(conv_b.reshape(1, 2 * D_FF).astype(F32))
    wd = w_down.astype(BF16).reshape(nc, tf, D_MODEL)
    wo = w_out.astype(BF16)
    return pl.pallas_call(
        functools.partial(_ffn_kernel, tiles_per_batch=tiles_per_batch),
        out_shape=jax.ShapeDtypeStruct((ntok, D_MODEL), F32),
        grid=(ntok // tm,),
        in_specs=[
            pl.BlockSpec((tm, D_MODEL), lambda i: (i, 0)),
            pl.BlockSpec((tm, D_SSM), lambda i: (i, 0)),
            pl.BlockSpec((tm, D_ATTN), lambda i: (i, 0)),
            pl.BlockSpec((None, N_MOD, D_MODEL), lambda i: (i // tiles_per_batch, 0, 0)),
            _const_spec((D_SSM, D_MODEL)),
            _const_spec((D_ATTN, D_MODEL)),
            _const_spec((1, D_MODEL)),
            _const_spec((nc, D_MODEL, tf)),
            _const_spec((nc, D_MODEL, tf)),
            _const_spec((nc, 3, tf)),
            _const_spec((nc, 3, tf)),
            _const_spec((nc, 1, tf)),
            _const_spec((nc, 1, tf)),
            _const_spec((nc, tf, D_MODEL)),
        ],
        out_specs=pl.BlockSpec((tm, D_MODEL), lambda i: (i, 0)),
        scratch_shapes=[
            pltpu.VMEM((tm, D_MODEL), BF16),
            pltpu.VMEM((tm, D_MODEL), F32),
            pltpu.VMEM((2 * nc, SUBLANES, tf), F32),
            pltpu.VMEM((2, tm, tf), F32),
            pltpu.VMEM((2, tm, tf), F32),
        ],
        compiler_params=_params(("arbitrary",)),
        name="out_proj_convffn",
    )(x2d, ys, ya, mod_l, wo[:D_SSM], wo[D_SSM:], n2w.reshape(1, D_MODEL).astype(F32),
      wv, wg, cwv, cwg, cbv, cbg, wd)


def _in_weights(w_in_l, qw, kw):
    wu = w_in_l[:, :D_SSM]
    wq = w_in_l[:, D_SSM:D_SSM + D_ATTN]
    wk = w_in_l[:, D_SSM + D_ATTN:D_SSM + D_ATTN + KV_DIM].reshape(D_MODEL, N_KV_HEADS, 1, HEAD_DIM)
    wv = w_in_l[:, D_SSM + D_ATTN + KV_DIM:].reshape(D_MODEL, N_KV_HEADS, 1, HEAD_DIM)
    wk2 = jnp.broadcast_to(wk, (D_MODEL, N_KV_HEADS, 2, HEAD_DIM)).reshape(D_MODEL, 2 * KV_DIM)
    wv2 = jnp.broadcast_to(wv, (D_MODEL, N_KV_HEADS, 2, HEAD_DIM)).reshape(D_MODEL, 2 * KV_DIM)
    w_ext = jnp.concatenate([wu, wq, wk2, wv2], axis=1).astype(BF16)
    qw_ext = (jnp.tile(qw.astype(F32), N_Q_HEADS) * (HEAD_DIM ** -0.5)).reshape(1, D_ATTN)
    kw_ext = jnp.tile(kw.astype(F32), 2 * N_KV_HEADS).reshape(1, 2 * KV_DIM)
    return w_ext, qw_ext, kw_ext


def kernel(x, c, w_mod, b_mod, norm1_w, w_in, lam_re, lam_im, log_dt, ssm_b_re, ssm_b_im, ssm_c_re,
           ssm_c_im, ssm_d, w_glu, b_glu, q_norm_w, k_norm_w, rel_bias, sinks, out_norm_ssm,
           out_norm_attn, w_out, norm2_w, w_up, conv_w, conv_b, w_down):
    bsz, seq, _ = x.shape
    assert seq % T_S5 == 0 and seq % TQ_ATTN == 0 and seq % TM_IN == 0 and seq % TM_FFN == 0
    mod = _modulation(c.astype(F32), w_mod.astype(F32), b_mod.astype(F32))
    mod = mod.reshape(DEPTH, bsz, N_MOD, D_MODEL)
    bias = _bias_table(rel_bias.astype(F32))
    x2d = x.reshape(bsz * seq, D_MODEL).astype(F32)
    for l in range(DEPTH):
        w_ext, qw_ext, kw_ext = _in_weights(w_in[l], q_norm_w[l], k_norm_w[l])
        u, q, kd, vd = _in_projection(x2d, mod[l], norm1_w[l].reshape(1, D_MODEL).astype(F32),
                                      w_ext, qw_ext, kw_ext, seq // TM_IN)
        ops = _s5_operands(lam_re[l], lam_im[l], log_dt[l], ssm_b_re[l], ssm_b_im[l],
                           ssm_c_re[l], ssm_c_im[l])
        ys = _s5(u, ops, ssm_d[l], w_glu[l], b_glu[l], out_norm_ssm[l], bsz, seq)
        ya = _attention(q, kd, vd, bias, sinks[l], out_norm_attn[l], bsz, seq)
        x2d = _out_ffn(x2d, ys, ya, mod[l], w_out[l], norm2_w[l], w_up[l], conv_w[l], conv_b[l],
                       w_down[l], seq // TM_FFN)
    return x2d.reshape(bsz, seq, D_MODEL).astype(x.dtype)
```

```python
import functools
import math

import numpy as np
import jax
import jax.numpy as jnp
from jax import lax
from jax.experimental import pallas as pl
from jax.experimental.pallas import tpu as pltpu

D_MODEL = 1024
DEPTH = 2
D_SSM = 512
SSM_GROUP = 16
N_SSM_GROUPS = D_SSM // SSM_GROUP
SSM_STATE = 64
N_STATE = N_SSM_GROUPS * SSM_STATE
D_ATTN = D_MODEL - D_SSM
HEAD_DIM = 64
N_Q_HEADS = D_ATTN // HEAD_DIM
N_KV_HEADS = 2
Q_PER_KV = N_Q_HEADS // N_KV_HEADS
KV_DIM = N_KV_HEADS * HEAD_DIM
WINDOW = 128
BLOCK = 128
N_BUCKETS = 32
MAX_DISTANCE = 128
D_FF = 2816
N_MOD = 6
EPS = 1e-6
NEG_INF = -1e30
LOG2E = math.log2(math.e)

SUBLANES = 8
LANES = 128
VMEM_LIMIT_BYTES = 56 * 1024 * 1024

TM_IN = 512
T_S5 = 512
SEG = T_S5 // SUBLANES
LW_S5 = 512
TQ_ATTN = 512
TM_FFN = 512
TF_FFN = 256
RB_FFN = 256
N_FF_CHUNKS = D_FF // TF_FFN
D_PROJ = D_SSM + D_ATTN + 4 * KV_DIM

F32 = jnp.float32
BF16 = jnp.bfloat16


def _params(semantics):
    return pltpu.CompilerParams(dimension_semantics=semantics, vmem_limit_bytes=VMEM_LIMIT_BYTES)


def _const_spec(shape):
    nd = len(shape)
    return pl.BlockSpec(shape, lambda *_: (0,) * nd, pipeline_mode=pl.Buffered(1))


def _mod_kernel(c_ref, w_ref, b_ref, o_ref):
    c = c_ref[...]
    c_act = c * (1.0 / (1.0 + jnp.exp(-c)))
    o_ref[...] = jnp.dot(c_act, w_ref[...], preferred_element_type=F32) + b_ref[...]


def _modulation(c, w_mod, b_mod):
    bsz = c.shape[0]
    tn = 1536
    n = N_MOD * D_MODEL
    return pl.pallas_call(
        _mod_kernel,
        out_shape=jax.ShapeDtypeStruct((DEPTH, bsz, n), F32),
        grid=(DEPTH, n // tn),
        in_specs=[
            pl.BlockSpec((bsz, D_MODEL), lambda l, j: (0, 0)),
            pl.BlockSpec((None, D_MODEL, tn), lambda l, j: (l, 0, j)),
            pl.BlockSpec((None, 1, tn), lambda l, j: (l, 0, j)),
        ],
        out_specs=pl.BlockSpec((None, bsz, tn), lambda l, j: (l, 0, j)),
        compiler_params=_params(("arbitrary", "arbitrary")),
        name="adaln_mod",
    )(c, w_mod, b_mod.reshape(DEPTH, 1, n))


def _t5_bucket(n):
    n = np.maximum(n, 0)
    max_exact = N_BUCKETS // 2
    log_part = np.log(np.maximum(n, 1) / max_exact) / math.log(MAX_DISTANCE / max_exact)
    large = max_exact + (log_part * (N_BUCKETS - max_exact)).astype(np.int32)
    large = np.minimum(large, N_BUCKETS - 1)
    return np.where(n < max_exact, n, large).astype(np.int32)


def _band_buckets():
    dist = (np.arange(BLOCK)[:, None] + BLOCK) - np.arange(2 * BLOCK)[None, :]
    valid = (dist >= 0) & (dist < WINDOW)
    return np.where(valid, _t5_bucket(dist), -1).astype(np.int32)


def _bias_kernel(rb_ref, bucket_ref, o_ref):
    h = pl.program_id(0)
    bucket = bucket_ref[...]
    acc = jnp.full(bucket.shape, NEG_INF, F32)
    for b in range(N_BUCKETS):
        acc = jnp.where(bucket == b, rb_ref[b, h] * LOG2E, acc)
    o_ref[...] = acc


def _bias_table(rel_bias):
    return pl.pallas_call(
        _bias_kernel,
        out_shape=jax.ShapeDtypeStruct((N_Q_HEADS, BLOCK, 2 * BLOCK), F32),
        grid=(N_Q_HEADS,),
        in_specs=[
            pl.BlockSpec(memory_space=pltpu.SMEM),
            pl.BlockSpec((BLOCK, 2 * BLOCK), lambda h: (0, 0)),
        ],
        out_specs=pl.BlockSpec((None, BLOCK, 2 * BLOCK), lambda h: (h, 0, 0)),
        compiler_params=_params(("arbitrary",)),
        name="rel_bias_table",
    )(rel_bias, jnp.asarray(_band_buckets()))


def _in_kernel(x_ref, mod_ref, n1w_ref, w_ref, segq_ref, segk_ref, qw_ref, kw_ref,
               u_ref, q_ref, k_ref, v_ref):
    x = x_ref[...]
    sh1 = mod_ref[0:1, :]
    sc1 = mod_ref[1:2, :]
    ms = jnp.mean(x * x, axis=-1, keepdims=True)
    h = (x * lax.rsqrt(ms + EPS) * n1w_ref[...]) * (1.0 + sc1) + sh1
    proj = jnp.dot(h.astype(BF16), w_ref[...], preferred_element_type=F32)
    u_ref[...] = proj[:, :D_SSM].astype(BF16)
    q = proj[:, D_SSM:D_SSM + D_ATTN]
    k = proj[:, D_SSM + D_ATTN:D_SSM + D_ATTN + 2 * KV_DIM]
    v = proj[:, D_SSM + D_ATTN + 2 * KV_DIM:]
    q_ms = jnp.dot((q * q).astype(BF16), segq_ref[...], preferred_element_type=F32) * (1.0 / HEAD_DIM)
    k_ms = jnp.dot((k * k).astype(BF16), segk_ref[...], preferred_element_type=F32) * (1.0 / HEAD_DIM)
    q_ref[...] = (q * lax.rsqrt(q_ms + EPS) * qw_ref[...]).astype(BF16)
    k_ref[...] = (k * lax.rsqrt(k_ms + EPS) * kw_ref[...]).astype(BF16)
    v_ref[...] = v.astype(BF16)


def _seg_ones(n):
    idx = np.arange(n) // HEAD_DIM
    return jnp.asarray((idx[:, None] == idx[None, :]).astype(np.float32), dtype=BF16)


def _in_projection(x2d, mod_l, n1w, w_ext, qw_ext, kw_ext, tiles_per_batch):
    ntok = x2d.shape[0]
    tm = TM_IN
    return pl.pallas_call(
        _in_kernel,
        out_shape=(
            jax.ShapeDtypeStruct((ntok, D_SSM), BF16),
            jax.ShapeDtypeStruct((ntok, D_ATTN), BF16),
            jax.ShapeDtypeStruct((ntok, 2 * KV_DIM), BF16),
            jax.ShapeDtypeStruct((ntok, 2 * KV_DIM), BF16),
        ),
        grid=(ntok // tm,),
        in_specs=[
            pl.BlockSpec((tm, D_MODEL), lambda i: (i, 0)),
            pl.BlockSpec((None, N_MOD, D_MODEL), lambda i: (i // tiles_per_batch, 0, 0)),
            _const_spec((1, D_MODEL)),
            _const_spec((D_MODEL, D_PROJ)),
            _const_spec((D_ATTN, D_ATTN)),
            _const_spec((2 * KV_DIM, 2 * KV_DIM)),
            _const_spec((1, D_ATTN)),
            _const_spec((1, 2 * KV_DIM)),
        ],
        out_specs=(
            pl.BlockSpec((tm, D_SSM), lambda i: (i, 0)),
            pl.BlockSpec((tm, D_ATTN), lambda i: (i, 0)),
            pl.BlockSpec((tm, 2 * KV_DIM), lambda i: (i, 0)),
            pl.BlockSpec((tm, 2 * KV_DIM), lambda i: (i, 0)),
        ),
        compiler_params=_params(("arbitrary",)),
        name="in_projection",
    )(x2d, mod_l, n1w, w_ext, _seg_ones(D_ATTN), _seg_ones(2 * KV_DIM), qw_ext, kw_ext)


def _cmul(ar, ai, br, bi):
    return ar * br - ai * bi, ar * bi + ai * br


def _s5_kernel(u_ref, perm_ref, permt_ref, b_ref, c_ref, d_ref, a_ref, g_ref,
               wglu_ref, bglu_ref, nw_ref, o_ref, sre_ref, sim_ref, sbre_ref, sbim_ref, carry_ref):
    @pl.when(pl.program_id(1) == 0)
    def _():
        carry_ref[...] = jnp.zeros_like(carry_ref)

    up = jnp.dot(perm_ref[...], u_ref[...], preferred_element_type=F32).astype(BF16)
    n_lb = N_STATE // LW_S5
    cw = D_SSM // n_lb
    row = lax.broadcasted_iota(jnp.int32, (SUBLANES, LW_S5), 0)

    def in_matmul(lb):
        lanes = slice(lb * LW_S5, (lb + 1) * LW_S5)
        bu = jnp.dot(up[:, lb * cw:(lb + 1) * cw], b_ref[lb], preferred_element_type=F32)
        sre_ref[:, lanes] = bu[:, :LW_S5]
        sim_ref[:, lanes] = bu[:, LW_S5:]

    def scan(lb):
        lanes = slice(lb * LW_S5, (lb + 1) * LW_S5)
        ar = a_ref[0, :, lanes]
        ai = a_ref[1, :, lanes]
        er = jnp.zeros((SUBLANES, LW_S5), F32)
        ei = er
        for r in range(SEG):
            rows = slice(r * SUBLANES, (r + 1) * SUBLANES)
            pr, pi = _cmul(ar, ai, er, ei)
            er = pr + sre_ref[rows, lanes]
            ei = pi + sim_ref[rows, lanes]
            sre_ref[rows, lanes] = er
            sim_ref[rows, lanes] = ei
        cin_r = jnp.broadcast_to(carry_ref[0:1, lanes], (SUBLANES, LW_S5))
        cin_i = jnp.broadcast_to(carry_ref[1:2, lanes], (SUBLANES, LW_S5))
        cr = jnp.where(row == 0, cin_r, pltpu.roll(er, 1, axis=0))
        ci = jnp.where(row == 0, cin_i, pltpu.roll(ei, 1, axis=0))
        for step, shift in enumerate((1, 2, 4)):
            gr = g_ref[2 * step, :, lanes]
            gi = g_ref[2 * step + 1, :, lanes]
            tr, ti = _cmul(gr, gi, pltpu.roll(cr, shift, axis=0), pltpu.roll(ci, shift, axis=0))
            cr = cr + tr
            ci = ci + ti
        nxt_r, nxt_i = _cmul(g_ref[6, :, lanes], g_ref[7, :, lanes], cr, ci)
        carry_ref[0:1, lanes] = (nxt_r + er)[SUBLANES - 1:SUBLANES, :]
        carry_ref[1:2, lanes] = (nxt_i + ei)[SUBLANES - 1:SUBLANES, :]
        qr, qi = cr, ci
        for r2 in range(SEG // 2):
            vr, vi = [], []
            for r in (2 * r2, 2 * r2 + 1):
                rows = slice(r * SUBLANES, (r + 1) * SUBLANES)
                qr, qi = _cmul(ar, ai, qr, qi)
                vr.append(sre_ref[rows, lanes] + qr)
                vi.append(sim_ref[rows, lanes] + qi)
            rows2 = slice(r2 * 2 * SUBLANES, (r2 + 1) * 2 * SUBLANES)
            sbre_ref[rows2, lanes] = jnp.concatenate(vr, axis=0).astype(BF16)
            sbim_ref[rows2, lanes] = jnp.concatenate(vi, axis=0).astype(BF16)

    def out_matmul(lb):
        lanes = slice(lb * LW_S5, (lb + 1) * LW_S5)
        y = jnp.dot(sbre_ref[:, lanes], c_ref[lb, 0], preferred_element_type=F32)
        return y + jnp.dot(sbim_ref[:, lanes], c_ref[lb, 1], preferred_element_type=F32)

    ys = []
    in_matmul(0)
    for lb in range(n_lb):
        if lb + 1 < n_lb:
            in_matmul(lb + 1)
        scan(lb)
        ys.append(out_matmul(lb))
    y = jnp.concatenate(ys, axis=1) + d_ref[...] * up.astype(F32)
    z = jax.nn.gelu(y)
    gate = jnp.dot(z.astype(BF16), wglu_ref[...], preferred_element_type=F32) + bglu_ref[...]
    out = z * (1.0 / (1.0 + jnp.exp(-gate)))
    ms = jnp.mean(out * out, axis=-1, keepdims=True)
    outn = (out * lax.rsqrt(ms + EPS) * nw_ref[...]).astype(BF16)
    o_ref[...] = jnp.dot(permt_ref[...], outn, preferred_element_type=F32).astype(BF16)


def _segment_perm():
    p = np.zeros((T_S5, T_S5), np.float32)
    for r in range(SEG):
        for s in range(SUBLANES):
            p[r * SUBLANES + s, s * SEG + r] = 1.0
    return p


def _s5_operands(lam_re, lam_im, log_dt, b_re, b_im, c_re, c_im):
    lr = jnp.minimum(lam_re.astype(F32), -1e-4)
    li = lam_im.astype(F32)
    dt = jnp.exp(log_dt.astype(F32))[:, None]
    mag = jnp.exp(dt * lr)
    a_re = mag * jnp.cos(dt * li)
    a_im = mag * jnp.sin(dt * li)
    den = lr * lr + li * li
    f_re = ((a_re - 1.0) * lr + a_im * li) / den
    f_im = (a_im * lr - (a_re - 1.0) * li) / den
    br = b_re.astype(F32)
    bi = b_im.astype(F32)
    bb_re = f_re[..., None] * br - f_im[..., None] * bi
    bb_im = f_re[..., None] * bi + f_im[..., None] * br

    n_lb = N_STATE // LW_S5
    gpb = N_SSM_GROUPS // n_lb
    eye = jnp.eye(gpb, dtype=F32)

    def in_blocks(bb):
        x = bb.reshape(n_lb, gpb, SSM_STATE, SSM_GROUP)
        return jnp.einsum("kgph,gq->kghqp", x, eye).reshape(n_lb, gpb * SSM_GROUP, LW_S5)

    def out_blocks(cc):
        x = cc.astype(F32).reshape(n_lb, gpb, SSM_GROUP, SSM_STATE)
        return jnp.einsum("jghp,gq->jgpqh", x, eye).reshape(n_lb, LW_S5, gpb * SSM_GROUP)

    b_mat = jnp.concatenate([in_blocks(bb_re), in_blocks(bb_im)], axis=-1).astype(BF16)
    c_mat = jnp.stack([out_blocks(c_re), -out_blocks(c_im)], axis=1).astype(BF16)

    def power(n):
        m = jnp.exp(n * (dt * lr))
        return (m * jnp.cos(n * (dt * li))).reshape(1, N_STATE), (m * jnp.sin(n * (dt * li))).reshape(1, N_STATE)

    def tiled(v, zero_rows=0):
        t = jnp.broadcast_to(v, (SUBLANES, N_STATE))
        keep = (np.arange(SUBLANES) >= zero_rows)[:, None]
        return jnp.where(keep, t, 0.0)

    a_tab = jnp.stack([tiled(a_re.reshape(1, N_STATE)), tiled(a_im.reshape(1, N_STATE))])
    g_rows = []
    for shift in (1, 2, 4):
        pr, pi = power(float(shift * SEG))
        g_rows += [tiled(pr, shift), tiled(pi, shift)]
    pr, pi = power(float(SEG))
    g_rows += [tiled(pr), tiled(pi)]
    return b_mat, c_mat, a_tab, jnp.stack(g_rows)


def _s5(u, ops, d_skip, w_glu, b_glu, nw, bsz, seq):
    b_mat, c_mat, a_tab, g_tab = ops
    t = T_S5
    tiles = seq // t
    perm = _segment_perm()
    return pl.pallas_call(
        _s5_kernel,
        out_shape=jax.ShapeDtypeStruct((bsz * seq, D_SSM), BF16),
        grid=(bsz, tiles),
        in_specs=[
            pl.BlockSpec((t, D_SSM), lambda b, i: (b * tiles + i, 0)),
            _const_spec((t, t)),
            _const_spec((t, t)),
            _const_spec(b_mat.shape),
            _const_spec(c_mat.shape),
            _const_spec((1, D_SSM)),
            _const_spec(a_tab.shape),
            _const_spec(g_tab.shape),
            _const_spec((D_SSM, D_SSM)),
            _const_spec((1, D_SSM)),
            _const_spec((1, D_SSM)),
        ],
        out_specs=pl.BlockSpec((t, D_SSM), lambda b, i: (b * tiles + i, 0)),
        scratch_shapes=[
            pltpu.VMEM((t, N_STATE), F32),
            pltpu.VMEM((t, N_STATE), F32),
            pltpu.VMEM((t, N_STATE), BF16),
            pltpu.VMEM((t, N_STATE), BF16),
            pltpu.VMEM((2, N_STATE), F32),
        ],
        compiler_params=_params(("arbitrary", "arbitrary")),
        name="s5_scan",
    )(u, jnp.asarray(perm, dtype=BF16), jnp.asarray(perm.T, dtype=BF16), b_mat, c_mat,
      d_skip.reshape(1, D_SSM).astype(F32), a_tab, g_tab, w_glu.astype(BF16),
      b_glu.reshape(1, D_SSM).astype(F32), nw.reshape(1, D_SSM).astype(F32))


def _attn_kernel(sink_ref, q_ref, kc_ref, kp_ref, vc_ref, vp_ref, bias_ref, nw_ref, o_ref,
                 lg_ref, p_ref, es_ref, y_ref):
    first = pl.program_id(1) == 0
    pw = 2 * HEAD_DIM
    lane = lax.broadcasted_iota(jnp.int32, (BLOCK, pw), 1)
    low = lane < HEAD_DIM
    col = lax.broadcasted_iota(jnp.int32, (Q_PER_KV * BLOCK, 2 * BLOCK), 1)
    start_mask = jnp.where(jnp.logical_and(first, col < BLOCK), NEG_INF, 0.0)
    ones = jnp.ones((2 * BLOCK, pw), BF16)
    zero = jnp.zeros((BLOCK, pw), BF16)
    n_items = (TQ_ATTN // BLOCK) * N_KV_HEADS

    def band(cur_ref, prev_ref, jb, g):
        kv_lanes = slice(g * pw, (g + 1) * pw)
        if jb == 0:
            return jnp.concatenate([prev_ref[:, kv_lanes], cur_ref[0:BLOCK, kv_lanes]], axis=0)
        return cur_ref[(jb - 1) * BLOCK:(jb + 1) * BLOCK, kv_lanes]

    def logits(it):
        jb, g = divmod(it, N_KV_HEADS)
        qb = q_ref[jb * BLOCK:(jb + 1) * BLOCK, :]
        stack = []
        for m in range(Q_PER_KV // 2):
            qp = qb[:, (g * 2 + m) * pw:(g * 2 + m + 1) * pw]
            stack += [jnp.where(low, qp, zero), jnp.where(low, zero, qp)]
        q4 = jnp.concatenate(stack, axis=0)
        lg = lax.dot_general(q4, band(kc_ref, kp_ref, jb, g), (((1,), (1,)), ((), ())),
                             preferred_element_type=F32)
        lg = lg + bias_ref[g * Q_PER_KV:(g + 1) * Q_PER_KV].reshape(Q_PER_KV * BLOCK, 2 * BLOCK)
        if jb == 0:
            lg = lg + start_mask
        lg_ref[it % 2] = lg

    def softmax(it):
        g = it % N_KV_HEADS
        lg = lg_ref[it % 2]
        sink = jnp.concatenate(
            [jnp.full((BLOCK, pw), sink_ref[g * Q_PER_KV + e], F32) for e in range(Q_PER_KV)], axis=0)
        mx = jnp.maximum(jnp.broadcast_to(jnp.max(lg, axis=-1, keepdims=True), sink.shape), sink)
        p_ref[it % 2, :, :pw] = jnp.exp2(lg[:, :pw] - mx).astype(BF16)
        p_ref[it % 2, :, pw:] = jnp.exp2(lg[:, pw:] - mx).astype(BF16)
        es_ref[it % 2] = jnp.exp2(sink - mx)

    def weighted_sum(it):
        jb, g = divmod(it, N_KV_HEADS)
        vext = jnp.concatenate([band(vc_ref, vp_ref, jb, g), ones], axis=1)
        pv = jnp.dot(p_ref[it % 2], vext, preferred_element_type=F32)
        o4 = pv[:, :pw] / (pv[:, pw:] + es_ref[it % 2])
        for m in range(Q_PER_KV // 2):
            y_ref[jb * BLOCK:(jb + 1) * BLOCK, (g * 2 + m) * pw:(g * 2 + m + 1) * pw] = jnp.where(
                low, o4[(2 * m) * BLOCK:(2 * m + 1) * BLOCK], o4[(2 * m + 1) * BLOCK:(2 * m + 2) * BLOCK])
        if g == N_KV_HEADS - 1:
            y = y_ref[jb * BLOCK:(jb + 1) * BLOCK, :]
            ms = jnp.mean(y * y, axis=-1, keepdims=True)
            o_ref[jb * BLOCK:(jb + 1) * BLOCK, :] = (y * lax.rsqrt(ms + EPS) * nw_ref[...]).astype(BF16)

    for t in range(n_items + 2):
        if t < n_items:
            logits(t)
        if 0 <= t - 1 < n_items:
            softmax(t - 1)
        if 0 <= t - 2 < n_items:
            weighted_sum(t - 2)


def _attention(q, kd, vd, bias, sinks, nw, bsz, seq):
    tq = TQ_ATTN
    tiles = seq // tq
    ratio = tq // BLOCK

    def prev_map(b, i):
        return (jnp.maximum((b * tiles + i) * ratio - 1, 0), 0)

    return pl.pallas_call(
        _attn_kernel,
        out_shape=jax.ShapeDtypeStruct((bsz * seq, D_ATTN), BF16),
        grid=(bsz, tiles),
        in_specs=[
            pl.BlockSpec(memory_space=pltpu.SMEM),
            pl.BlockSpec((tq, D_ATTN), lambda b, i: (b * tiles + i, 0)),
            pl.BlockSpec((tq, 2 * KV_DIM), lambda b, i: (b * tiles + i, 0)),
            pl.BlockSpec((BLOCK, 2 * KV_DIM), prev_map),
            pl.BlockSpec((tq, 2 * KV_DIM), lambda b, i: (b * tiles + i, 0)),
            pl.BlockSpec((BLOCK, 2 * KV_DIM), prev_map),
            _const_spec((N_Q_HEADS, BLOCK, 2 * BLOCK)),
            _const_spec((1, D_ATTN)),
        ],
        out_specs=pl.BlockSpec((tq, D_ATTN), lambda b, i: (b * tiles + i, 0)),
        scratch_shapes=[
            pltpu.VMEM((2, Q_PER_KV * BLOCK, 2 * BLOCK), F32),
            pltpu.VMEM((2, Q_PER_KV * BLOCK, 2 * BLOCK), BF16),
            pltpu.VMEM((2, Q_PER_KV * BLOCK, 2 * HEAD_DIM), F32),
            pltpu.VMEM((tq, D_ATTN), F32),
        ],
        compiler_params=_params(("arbitrary", "arbitrary")),
        name="swa_attention",
    )(sinks.astype(F32) * LOG2E, q, kd, kd, vd, vd, bias, nw.reshape(1, D_ATTN).astype(F32))


def _ffn_kernel(x_ref, ys_ref, ya_ref, mod_ref, wos_ref, woa_ref, n2w_ref, wv_ref, wg_ref,
                cwv_ref, cwg_ref, cbv_ref, cbg_ref, wd_ref, o_ref, h_ref, acc_ref, tail_ref,
                upv_ref, upg_ref, *, tiles_per_batch):
    first = (pl.program_id(0) % tiles_per_batch) == 0
    g1 = mod_ref[2:3, :]
    sh2 = mod_ref[3:4, :]
    sc2 = mod_ref[4:5, :]
    g2 = mod_ref[5:6, :]
    tm = x_ref.shape[0]
    n_rb = tm // RB_FFN
    for rb in range(n_rb):
        rows = slice(rb * RB_FFN, (rb + 1) * RB_FFN)
        mixed = jnp.dot(ys_ref[rows, :], wos_ref[...], preferred_element_type=F32)
        mixed = mixed + jnp.dot(ya_ref[rows, :], woa_ref[...], preferred_element_type=F32)
        x1 = x_ref[rows, :] + g1 * mixed
        o_ref[rows, :] = x1
        ms = jnp.mean(x1 * x1, axis=-1, keepdims=True)
        h_ref[rows, :] = ((x1 * lax.rsqrt(ms + EPS) * n2w_ref[...]) * (1.0 + sc2) + sh2).astype(BF16)
    acc_ref[...] = jnp.zeros_like(acc_ref)

    def conv(up_ref, slot, rb, cw, cb, tail_slot):
        up = up_ref[slot, rb * RB_FFN:(rb + 1) * RB_FFN, :]
        if rb == 0:
            prev = jnp.where(first, 0.0, tail_ref[tail_slot])
        else:
            prev = up_ref[slot, rb * RB_FFN - SUBLANES:rb * RB_FFN, :]
        if rb == n_rb - 1:
            tail_ref[tail_slot] = up[RB_FFN - SUBLANES:, :]
        ext = jnp.concatenate([prev, up], axis=0)
        up1 = pltpu.roll(ext, 1, axis=0)[SUBLANES:, :]
        up2 = pltpu.roll(ext, 2, axis=0)[SUBLANES:, :]
        return cw[0:1, :] * up2 + cw[1:2, :] * up1 + cw[2:3, :] * up + cb

    def up_proj(j, slot, rb):
        rows = slice(rb * RB_FFN, (rb + 1) * RB_FFN)
        h = h_ref[rows, :]
        upv_ref[slot, rows, :] = jnp.dot(h, wv_ref[j], preferred_element_type=F32)
        upg_ref[slot, rows, :] = jnp.dot(h, wg_ref[j], preferred_element_type=F32)

    def consume(j, slot, rb):
        rows = slice(rb * RB_FFN, (rb + 1) * RB_FFN)
        val = conv(upv_ref, slot, rb, cwv_ref[j], cbv_ref[j], 2 * j)
        gate = conv(upg_ref, slot, rb, cwg_ref[j], cbg_ref[j], 2 * j + 1)
        act = (gate * (1.0 / (1.0 + jnp.exp(-gate))) * val).astype(BF16)
        acc_ref[rows, :] += jnp.dot(act, wd_ref[j], preferred_element_type=F32)

    for rb in range(n_rb):
        up_proj(0, 0, rb)

    def two_chunks(jj, carry):
        j = 2 * jj
        for rb in range(n_rb):
            up_proj(j + 1, 1, rb)
            consume(j, 0, rb)
        for rb in range(n_rb):
            up_proj(j + 2, 0, rb)
            consume(j + 1, 1, rb)
        return carry

    lax.fori_loop(0, (N_FF_CHUNKS - 1) // 2, two_chunks, 0)
    for rb in range(n_rb):
        consume(N_FF_CHUNKS - 1, 0, rb)
    o_ref[...] = o_ref[...] + g2 * acc_ref[...]


def _out_ffn(x2d, ys, ya, mod_l, w_out, n2w, w_up, conv_w, conv_b, w_down, tiles_per_batch):
    ntok = x2d.shape[0]
    tm = TM_FFN
    nc, tf = N_FF_CHUNKS, TF_FFN

    def cols(a):
        r = a.shape[0]
        v = a[:, :D_FF].reshape(r, nc, tf).transpose(1, 0, 2)
        g = a[:, D_FF:].reshape(r, nc, tf).transpose(1, 0, 2)
        return v, g

    wv, wg = cols(w_up.astype(BF16))
    cwv, cwg = cols(conv_w.astype(F32))
    cbv, cbg = cols(conv_b.reshape(1, 2 * D_FF).astype(F32))
    wd = w_down.astype(BF16).reshape(nc, tf, D_MODEL)
    wo = w_out.astype(BF16)
    return pl.pallas_call(
        functools.partial(_ffn_kernel, tiles_per_batch=tiles_per_batch),
        out_shape=jax.ShapeDtypeStruct((ntok, D_MODEL), F32),
        grid=(ntok // tm,),
        in_specs=[
            pl.BlockSpec((tm, D_MODEL), lambda i: (i, 0)),
            pl.BlockSpec((tm, D_SSM), lambda i: (i, 0)),
            pl.BlockSpec((tm, D_ATTN), lambda i: (i, 0)),
            pl.BlockSpec((None, N_MOD, D_MODEL), lambda i: (i // tiles_per_batch, 0, 0)),
            _const_spec((D_SSM, D_MODEL)),
            _const_spec((D_ATTN, D_MODEL)),
            _const_spec((1, D_MODEL)),
            _const_spec((nc, D_MODEL, tf)),
            _const_spec((nc, D_MODEL, tf)),
            _const_spec((nc, 3, tf)),
            _const_spec((nc, 3, tf)),
            _const_spec((nc, 1, tf)),
            _const_spec((nc, 1, tf)),
            _const_spec((nc, tf, D_MODEL)),
        ],
        out_specs=pl.BlockSpec((tm, D_MODEL), lambda i: (i, 0)),
        scratch_shapes=[
            pltpu.VMEM((tm, D_MODEL), BF16),
            pltpu.VMEM((tm, D_MODEL), F32),
            pltpu.VMEM((2 * nc, SUBLANES, tf), F32),
            pltpu.VMEM((2, tm, tf), F32),
            pltpu.VMEM((2, tm, tf), F32),
        ],
        compiler_params=_params(("arbitrary",)),
        name="out_proj_convffn",
    )(x2d, ys, ya, mod_l, wo[:D_SSM], wo[D_SSM:], n2w.reshape(1, D_MODEL).astype(F32),
      wv, wg, cwv, cwg, cbv, cbg, wd)


def _in_weights(w_in_l, qw, kw):
    wu = w_in_l[:, :D_SSM]
    wq = w_in_l[:, D_SSM:D_SSM + D_ATTN]
    wk = w_in_l[:, D_SSM + D_ATTN:D_SSM + D_ATTN + KV_DIM].reshape(D_MODEL, N_KV_HEADS, 1, HEAD_DIM)
    wv = w_in_l[:, D_SSM + D_ATTN + KV_DIM:].reshape(D_MODEL, N_KV_HEADS, 1, HEAD_DIM)
    wk2 = jnp.broadcast_to(wk, (D_MODEL, N_KV_HEADS, 2, HEAD_DIM)).reshape(D_MODEL, 2 * KV_DIM)
    wv2 = jnp.broadcast_to(wv, (D_MODEL, N_KV_HEADS, 2, HEAD_DIM)).reshape(D_MODEL, 2 * KV_DIM)
    w_ext = jnp.concatenate([wu, wq, wk2, wv2], axis=1).astype(BF16)
    qw_ext = (jnp.tile(qw.astype(F32), N_Q_HEADS) * (HEAD_DIM ** -0.5 * LOG2E)).reshape(1, D_ATTN)
    kw_ext = jnp.tile(kw.astype(F32), 2 * N_KV_HEADS).reshape(1, 2 * KV_DIM)
    return w_ext, qw_ext, kw_ext


def kernel(x, c, w_mod, b_mod, norm1_w, w_in, lam_re, lam_im, log_dt, ssm_b_re, ssm_b_im, ssm_c_re,
           ssm_c_im, ssm_d, w_glu, b_glu, q_norm_w, k_norm_w, rel_bias, sinks, out_norm_ssm,
           out_norm_attn, w_out, norm2_w, w_up, conv_w, conv_b, w_down):
    bsz, seq, _ = x.shape
    assert seq % T_S5 == 0 and seq % TQ_ATTN == 0 and seq % TM_IN == 0 and seq % TM_FFN == 0
    mod = _modulation(c.astype(F32), w_mod.astype(F32), b_mod.astype(F32))
    mod = mod.reshape(DEPTH, bsz, N_MOD, D_MODEL)
    bias = _bias_table(rel_bias.astype(F32))
    x2d = x.reshape(bsz * seq, D_MODEL).astype(F32)
    for l in range(DEPTH):
        w_ext, qw_ext, kw_ext = _in_weights(w_in[l], q_norm_w[l], k_norm_w[l])
        u, q, kd, vd = _in_projection(x2d, mod[l], norm1_w[l].reshape(1, D_MODEL).astype(F32),
                                      w_ext, qw_ext, kw_ext, seq // TM_IN)
        ops = _s5_operands(lam_re[l], lam_im[l], log_dt[l], ssm_b_re[l], ssm_b_im[l],
                           ssm_c_re[l], ssm_c_im[l])
        ys = _s5(u, ops, ssm_d[l], w_glu[l], b_glu[l], out_norm_ssm[l], bsz, seq)
        ya = _attention(q, kd, vd, bias, sinks[l], out_norm_attn[l], bsz, seq)
        x2d = _out_ffn(x2d, ys, ya, mod[l], w_out[l], norm2_w[l], w_up[l], conv_w[l], conv_b[l],
                       w_down[l], seq // TM_FFN)
    return x2d.reshape(bsz, seq, D_MODEL).astype(x.dtype)
```

```python
import functools
import math

import numpy as np
import jax
import jax.numpy as jnp
from jax import lax
from jax.experimental import pallas as pl
from jax.experimental.pallas import tpu as pltpu

D_MODEL = 1024
DEPTH = 2
D_SSM = 512
SSM_GROUP = 16
N_SSM_GROUPS = D_SSM // SSM_GROUP
SSM_STATE = 64
N_STATE = N_SSM_GROUPS * SSM_STATE
D_ATTN = D_MODEL - D_SSM
HEAD_DIM = 64
N_Q_HEADS = D_ATTN // HEAD_DIM
N_KV_HEADS = 2
Q_PER_KV = N_Q_HEADS // N_KV_HEADS
KV_DIM = N_KV_HEADS * HEAD_DIM
WINDOW = 128
BLOCK = 128
N_BUCKETS = 32
MAX_DISTANCE = 128
D_FF = 2816
N_MOD = 6
EPS = 1e-6
NEG_INF = -1e30
LOG2E = math.log2(math.e)

SUBLANES = 8
LANES = 128
VMEM_LIMIT_BYTES = 56 * 1024 * 1024

TM_IN = 512
T_S5 = 512
SEG = T_S5 // SUBLANES
LW_S5 = 512
TQ_ATTN = 512
TM_FFN = 512
TF_FFN = 256
N_FF_CHUNKS = D_FF // TF_FFN
D_PROJ = D_SSM + D_ATTN + 4 * KV_DIM

F32 = jnp.float32
BF16 = jnp.bfloat16


def _params(semantics):
    return pltpu.CompilerParams(dimension_semantics=semantics, vmem_limit_bytes=VMEM_LIMIT_BYTES)


def _const_spec(shape):
    nd = len(shape)
    return pl.BlockSpec(shape, lambda *_: (0,) * nd, pipeline_mode=pl.Buffered(1))


def _layer_spec(shape, *lead):
    nd = len(shape)
    return pl.BlockSpec((None,) * len(lead) + tuple(shape), lambda *_: tuple(lead) + (0,) * nd,
                        pipeline_mode=pl.Buffered(1))


def _mod_kernel(c_ref, w_ref, b_ref, o_ref):
    c = c_ref[...]
    c_act = c * (1.0 / (1.0 + jnp.exp(-c)))
    o_ref[...] = jnp.dot(c_act, w_ref[...], preferred_element_type=F32) + b_ref[...]


def _modulation(c, w_mod, b_mod):
    bsz = c.shape[0]
    tn = 1536
    n = N_MOD * D_MODEL
    return pl.pallas_call(
        _mod_kernel,
        out_shape=jax.ShapeDtypeStruct((DEPTH, bsz, n), F32),
        grid=(DEPTH, n // tn),
        in_specs=[
            pl.BlockSpec((bsz, D_MODEL), lambda l, j: (0, 0)),
            pl.BlockSpec((None, D_MODEL, tn), lambda l, j: (l, 0, j)),
            pl.BlockSpec((None, 1, tn), lambda l, j: (l, 0, j)),
        ],
        out_specs=pl.BlockSpec((None, bsz, tn), lambda l, j: (l, 0, j)),
        compiler_params=_params(("arbitrary", "arbitrary")),
        name="adaln_mod",
    )(c, w_mod, b_mod.reshape(DEPTH, 1, n))


def _t5_bucket(n):
    n = np.maximum(n, 0)
    max_exact = N_BUCKETS // 2
    log_part = np.log(np.maximum(n, 1) / max_exact) / math.log(MAX_DISTANCE / max_exact)
    large = max_exact + (log_part * (N_BUCKETS - max_exact)).astype(np.int32)
    large = np.minimum(large, N_BUCKETS - 1)
    return np.where(n < max_exact, n, large).astype(np.int32)


def _band_buckets():
    dist = (np.arange(BLOCK)[:, None] + BLOCK) - np.arange(2 * BLOCK)[None, :]
    valid = (dist >= 0) & (dist < WINDOW)
    return np.where(valid, _t5_bucket(dist), -1).astype(np.int32)


def _bias_kernel(rb_ref, bucket_ref, o_ref):
    h = pl.program_id(0)
    bucket = bucket_ref[...]
    acc = jnp.full(bucket.shape, NEG_INF, F32)
    for b in range(N_BUCKETS):
        acc = jnp.where(bucket == b, rb_ref[b, h] * LOG2E, acc)
    o_ref[...] = acc


def _bias_table(rel_bias):
    return pl.pallas_call(
        _bias_kernel,
        out_shape=jax.ShapeDtypeStruct((N_Q_HEADS, BLOCK, 2 * BLOCK), F32),
        grid=(N_Q_HEADS,),
        in_specs=[
            pl.BlockSpec(memory_space=pltpu.SMEM),
            pl.BlockSpec((BLOCK, 2 * BLOCK), lambda h: (0, 0)),
        ],
        out_specs=pl.BlockSpec((None, BLOCK, 2 * BLOCK), lambda h: (h, 0, 0)),
        compiler_params=_params(("arbitrary",)),
        name="rel_bias_table",
    )(rel_bias, jnp.asarray(_band_buckets()))


def _in_kernel(x_ref, mod_ref, n1w_ref, w_ref, segq_ref, segk_ref, qw_ref, kw_ref,
               u_ref, q_ref, k_ref, v_ref):
    x = x_ref[...]
    sh1 = mod_ref[0:1, :]
    sc1 = mod_ref[1:2, :]
    ms = jnp.mean(x * x, axis=-1, keepdims=True)
    h = (x * lax.rsqrt(ms + EPS) * n1w_ref[...]) * (1.0 + sc1) + sh1
    proj = jnp.dot(h.astype(BF16), w_ref[...], preferred_element_type=F32)
    u_ref[...] = proj[:, :D_SSM].astype(BF16)
    q = proj[:, D_SSM:D_SSM + D_ATTN]
    k = proj[:, D_SSM + D_ATTN:D_SSM + D_ATTN + 2 * KV_DIM]
    v = proj[:, D_SSM + D_ATTN + 2 * KV_DIM:]
    q_ms = jnp.dot((q * q).astype(BF16), segq_ref[...], preferred_element_type=F32) * (1.0 / HEAD_DIM)
    k_ms = jnp.dot((k * k).astype(BF16), segk_ref[...], preferred_element_type=F32) * (1.0 / HEAD_DIM)
    q_ref[...] = (q * lax.rsqrt(q_ms + EPS) * qw_ref[...]).astype(BF16)
    k_ref[...] = (k * lax.rsqrt(k_ms + EPS) * kw_ref[...]).astype(BF16)
    v_ref[...] = v.astype(BF16)


def _seg_ones(n):
    idx = np.arange(n) // HEAD_DIM
    return jnp.asarray((idx[:, None] == idx[None, :]).astype(np.float32), dtype=BF16)


def _in_projection(x2d, layer, mod, n1w, w_ext, qw_ext, kw_ext, tiles_per_batch):
    ntok = x2d.shape[0]
    tm = TM_IN
    return pl.pallas_call(
        _in_kernel,
        out_shape=(
            jax.ShapeDtypeStruct((ntok, D_SSM), BF16),
            jax.ShapeDtypeStruct((ntok, D_ATTN), BF16),
            jax.ShapeDtypeStruct((ntok, 2 * KV_DIM), BF16),
            jax.ShapeDtypeStruct((ntok, 2 * KV_DIM), BF16),
        ),
        grid=(ntok // tm,),
        in_specs=[
            pl.BlockSpec((tm, D_MODEL), lambda i: (i, 0)),
            pl.BlockSpec((None, None, N_MOD, D_MODEL), lambda i: (layer, i // tiles_per_batch, 0, 0)),
            _layer_spec((1, D_MODEL), layer),
            _layer_spec((D_MODEL, D_PROJ), layer),
            _const_spec((D_ATTN, D_ATTN)),
            _const_spec((2 * KV_DIM, 2 * KV_DIM)),
            _layer_spec((1, D_ATTN), layer),
            _layer_spec((1, 2 * KV_DIM), layer),
        ],
        out_specs=(
            pl.BlockSpec((tm, D_SSM), lambda i: (i, 0)),
            pl.BlockSpec((tm, D_ATTN), lambda i: (i, 0)),
            pl.BlockSpec((tm, 2 * KV_DIM), lambda i: (i, 0)),
            pl.BlockSpec((tm, 2 * KV_DIM), lambda i: (i, 0)),
        ),
        compiler_params=_params(("arbitrary",)),
        name="in_projection",
    )(x2d, mod, n1w, w_ext, _seg_ones(D_ATTN), _seg_ones(2 * KV_DIM), qw_ext, kw_ext)


def _cmul(ar, ai, br, bi):
    return ar * br - ai * bi, ar * bi + ai * br


def _s5_kernel(u_ref, perm_ref, permt_ref, b_ref, c_ref, d_ref, a_ref, g_ref,
               wglu_ref, bglu_ref, nw_ref, o_ref, sre_ref, sim_ref, sbre_ref, sbim_ref, carry_ref):
    @pl.when(pl.program_id(1) == 0)
    def _():
        carry_ref[...] = jnp.zeros_like(carry_ref)

    up = jnp.dot(perm_ref[...], u_ref[...], preferred_element_type=F32).astype(BF16)
    n_lb = N_STATE // LW_S5
    cw = D_SSM // n_lb
    row = lax.broadcasted_iota(jnp.int32, (SUBLANES, LW_S5), 0)

    def in_matmul(lb):
        lanes = slice(lb * LW_S5, (lb + 1) * LW_S5)
        bu = jnp.dot(up[:, lb * cw:(lb + 1) * cw], b_ref[lb], preferred_element_type=F32)
        sre_ref[:, lanes] = bu[:, :LW_S5]
        sim_ref[:, lanes] = bu[:, LW_S5:]

    def scan(lb):
        lanes = slice(lb * LW_S5, (lb + 1) * LW_S5)
        ar = a_ref[0, :, lanes]
        ai = a_ref[1, :, lanes]
        er = jnp.zeros((SUBLANES, LW_S5), F32)
        ei = er
        for r in range(SEG):
            rows = slice(r * SUBLANES, (r + 1) * SUBLANES)
            pr, pi = _cmul(ar, ai, er, ei)
            er = pr + sre_ref[rows, lanes]
            ei = pi + sim_ref[rows, lanes]
            sre_ref[rows, lanes] = er
            sim_ref[rows, lanes] = ei
        cin_r = jnp.broadcast_to(carry_ref[0:1, lanes], (SUBLANES, LW_S5))
        cin_i = jnp.broadcast_to(carry_ref[1:2, lanes], (SUBLANES, LW_S5))
        cr = jnp.where(row == 0, cin_r, pltpu.roll(er, 1, axis=0))
        ci = jnp.where(row == 0, cin_i, pltpu.roll(ei, 1, axis=0))
        for step, shift in enumerate((1, 2, 4)):
            gr = g_ref[2 * step, :, lanes]
            gi = g_ref[2 * step + 1, :, lanes]
            tr, ti = _cmul(gr, gi, pltpu.roll(cr, shift, axis=0), pltpu.roll(ci, shift, axis=0))
            cr = cr + tr
            ci = ci + ti
        nxt_r, nxt_i = _cmul(g_ref[6, :, lanes], g_ref[7, :, lanes], cr, ci)
        carry_ref[0:1, lanes] = (nxt_r + er)[SUBLANES - 1:SUBLANES, :]
        carry_ref[1:2, lanes] = (nxt_i + ei)[SUBLANES - 1:SUBLANES, :]
        qr, qi = cr, ci
        for r2 in range(SEG // 2):
            vr, vi = [], []
            for r in (2 * r2, 2 * r2 + 1):
                rows = slice(r * SUBLANES, (r + 1) * SUBLANES)
                qr, qi = _cmul(ar, ai, qr, qi)
                vr.append(sre_ref[rows, lanes] + qr)
                vi.append(sim_ref[rows, lanes] + qi)
            rows2 = slice(r2 * 2 * SUBLANES, (r2 + 1) * 2 * SUBLANES)
            sbre_ref[rows2, lanes] = jnp.concatenate(vr, axis=0).astype(BF16)
            sbim_ref[rows2, lanes] = jnp.concatenate(vi, axis=0).astype(BF16)

    def out_matmul(lb):
        lanes = slice(lb * LW_S5, (lb + 1) * LW_S5)
        y = jnp.dot(sbre_ref[:, lanes], c_ref[lb, 0], preferred_element_type=F32)
        return y + jnp.dot(sbim_ref[:, lanes], c_ref[lb, 1], preferred_element_type=F32)

    ys = []
    in_matmul(0)
    for lb in range(n_lb):
        if lb + 1 < n_lb:
            in_matmul(lb + 1)
        scan(lb)
        ys.append(out_matmul(lb))
    y = jnp.concatenate(ys, axis=1) + d_ref[...] * up.astype(F32)
    z = jax.nn.gelu(y)
    gate = jnp.dot(z.astype(BF16), wglu_ref[...], preferred_element_type=F32) + bglu_ref[...]
    out = z * (1.0 / (1.0 + jnp.exp(-gate)))
    ms = jnp.mean(out * out, axis=-1, keepdims=True)
    outn = (out * lax.rsqrt(ms + EPS) * nw_ref[...]).astype(BF16)
    o_ref[...] = jnp.dot(permt_ref[...], outn, preferred_element_type=F32).astype(BF16)


def _segment_perm():
    p = np.zeros((T_S5, T_S5), np.float32)
    for r in range(SEG):
        for s in range(SUBLANES):
            p[r * SUBLANES + s, s * SEG + r] = 1.0
    return p


def _s5_operands(lam_re, lam_im, log_dt, b_re, b_im, c_re, c_im):
    lr = jnp.minimum(lam_re.astype(F32), -1e-4)
    li = lam_im.astype(F32)
    dt = jnp.exp(log_dt.astype(F32))[:, None]
    mag = jnp.exp(dt * lr)
    a_re = mag * jnp.cos(dt * li)
    a_im = mag * jnp.sin(dt * li)
    den = lr * lr + li * li
    f_re = ((a_re - 1.0) * lr + a_im * li) / den
    f_im = (a_im * lr - (a_re - 1.0) * li) / den
    br = b_re.astype(F32)
    bi = b_im.astype(F32)
    bb_re = f_re[..., None] * br - f_im[..., None] * bi
    bb_im = f_re[..., None] * bi + f_im[..., None] * br

    n_lb = N_STATE // LW_S5
    gpb = N_SSM_GROUPS // n_lb
    eye = jnp.eye(gpb, dtype=F32)

    def in_blocks(bb):
        x = bb.reshape(n_lb, gpb, SSM_STATE, SSM_GROUP)
        return jnp.einsum("kgph,gq->kghqp", x, eye).reshape(n_lb, gpb * SSM_GROUP, LW_S5)

    def out_blocks(cc):
        x = cc.astype(F32).reshape(n_lb, gpb, SSM_GROUP, SSM_STATE)
        return jnp.einsum("jghp,gq->jgpqh", x, eye).reshape(n_lb, LW_S5, gpb * SSM_GROUP)

    b_mat = jnp.concatenate([in_blocks(bb_re), in_blocks(bb_im)], axis=-1).astype(BF16)
    c_mat = jnp.stack([out_blocks(c_re), -out_blocks(c_im)], axis=1).astype(BF16)

    def power(n):
        m = jnp.exp(n * (dt * lr))
        return (m * jnp.cos(n * (dt * li))).reshape(1, N_STATE), (m * jnp.sin(n * (dt * li))).reshape(1, N_STATE)

    def tiled(v, zero_rows=0):
        t = jnp.broadcast_to(v, (SUBLANES, N_STATE))
        keep = (np.arange(SUBLANES) >= zero_rows)[:, None]
        return jnp.where(keep, t, 0.0)

    a_tab = jnp.stack([tiled(a_re.reshape(1, N_STATE)), tiled(a_im.reshape(1, N_STATE))])
    g_rows = []
    for shift in (1, 2, 4):
        pr, pi = power(float(shift * SEG))
        g_rows += [tiled(pr, shift), tiled(pi, shift)]
    pr, pi = power(float(SEG))
    g_rows += [tiled(pr), tiled(pi)]
    return b_mat, c_mat, a_tab, jnp.stack(g_rows)


def _s5(u, layer, ops, d_skip, w_glu, b_glu, nw, bsz, seq):
    b_mat, c_mat, a_tab, g_tab = ops
    t = T_S5
    tiles = seq // t
    perm = _segment_perm()
    return pl.pallas_call(
        _s5_kernel,
        out_shape=jax.ShapeDtypeStruct((bsz * seq, D_SSM), BF16),
        grid=(bsz, tiles),
        in_specs=[
            pl.BlockSpec((t, D_SSM), lambda b, i: (b * tiles + i, 0)),
            _const_spec((t, t)),
            _const_spec((t, t)),
            _layer_spec(b_mat.shape[1:], layer),
            _layer_spec(c_mat.shape[1:], layer),
            _layer_spec((1, D_SSM), layer),
            _layer_spec(a_tab.shape[1:], layer),
            _layer_spec(g_tab.shape[1:], layer),
            _layer_spec((D_SSM, D_SSM), layer),
            _layer_spec((1, D_SSM), layer),
            _layer_spec((1, D_SSM), layer),
        ],
        out_specs=pl.BlockSpec((t, D_SSM), lambda b, i: (b * tiles + i, 0)),
        scratch_shapes=[
            pltpu.VMEM((t, N_STATE), F32),
            pltpu.VMEM((t, N_STATE), F32),
            pltpu.VMEM((t, N_STATE), BF16),
            pltpu.VMEM((t, N_STATE), BF16),
            pltpu.VMEM((2, N_STATE), F32),
        ],
        compiler_params=_params(("arbitrary", "arbitrary")),
        name="s5_scan",
    )(u, jnp.asarray(perm, dtype=BF16), jnp.asarray(perm.T, dtype=BF16), b_mat, c_mat,
      d_skip, a_tab, g_tab, w_glu, b_glu, nw)


def _attn_kernel(sink_ref, q_ref, kc_ref, kp_ref, vc_ref, vp_ref, bias_ref, nw_ref, o_ref,
                 lg_ref, p_ref, es_ref, y_ref, *, layer):
    first = pl.program_id(1) == 0
    pw = 2 * HEAD_DIM
    lane = lax.broadcasted_iota(jnp.int32, (BLOCK, pw), 1)
    low = lane < HEAD_DIM
    col = lax.broadcasted_iota(jnp.int32, (Q_PER_KV * BLOCK, 2 * BLOCK), 1)
    start_mask = jnp.where(jnp.logical_and(first, col < BLOCK), NEG_INF, 0.0)
    ones = jnp.ones((2 * BLOCK, pw), BF16)
    zero = jnp.zeros((BLOCK, pw), BF16)
    n_items = (TQ_ATTN // BLOCK) * N_KV_HEADS

    def band(cur_ref, prev_ref, jb, g):
        kv_lanes = slice(g * pw, (g + 1) * pw)
        if jb == 0:
            return jnp.concatenate([prev_ref[:, kv_lanes], cur_ref[0:BLOCK, kv_lanes]], axis=0)
        return cur_ref[(jb - 1) * BLOCK:(jb + 1) * BLOCK, kv_lanes]

    def logits(it):
        jb, g = divmod(it, N_KV_HEADS)
        qb = q_ref[jb * BLOCK:(jb + 1) * BLOCK, :]
        stack = []
        for m in range(Q_PER_KV // 2):
            qp = qb[:, (g * 2 + m) * pw:(g * 2 + m + 1) * pw]
            stack += [jnp.where(low, qp, zero), jnp.where(low, zero, qp)]
        q4 = jnp.concatenate(stack, axis=0)
        lg = lax.dot_general(q4, band(kc_ref, kp_ref, jb, g), (((1,), (1,)), ((), ())),
                             preferred_element_type=F32)
        lg = lg + bias_ref[g * Q_PER_KV:(g + 1) * Q_PER_KV].reshape(Q_PER_KV * BLOCK, 2 * BLOCK)
        if jb == 0:
            lg = lg + start_mask
        lg_ref[it % 2] = lg

    def softmax(it):
        g = it % N_KV_HEADS
        lg = lg_ref[it % 2]
        sink = jnp.concatenate(
            [jnp.full((BLOCK, pw), sink_ref[layer, g * Q_PER_KV + e], F32) for e in range(Q_PER_KV)], axis=0)
        mx = jnp.maximum(jnp.broadcast_to(jnp.max(lg, axis=-1, keepdims=True), sink.shape), sink)
        p_ref[it % 2, :, :pw] = jnp.exp2(lg[:, :pw] - mx).astype(BF16)
        p_ref[it % 2, :, pw:] = jnp.exp2(lg[:, pw:] - mx).astype(BF16)
        es_ref[it % 2] = jnp.exp2(sink - mx)

    def weighted_sum(it):
        jb, g = divmod(it, N_KV_HEADS)
        vext = jnp.concatenate([band(vc_ref, vp_ref, jb, g), ones], axis=1)
        pv = jnp.dot(p_ref[it % 2], vext, preferred_element_type=F32)
        o4 = pv[:, :pw] / (pv[:, pw:] + es_ref[it % 2])
        for m in range(Q_PER_KV // 2):
            y_ref[jb * BLOCK:(jb + 1) * BLOCK, (g * 2 + m) * pw:(g * 2 + m + 1) * pw] = jnp.where(
                low, o4[(2 * m) * BLOCK:(2 * m + 1) * BLOCK], o4[(2 * m + 1) * BLOCK:(2 * m + 2) * BLOCK])
        if g == N_KV_HEADS - 1:
            y = y_ref[jb * BLOCK:(jb + 1) * BLOCK, :]
            ms = jnp.mean(y * y, axis=-1, keepdims=True)
            o_ref[jb * BLOCK:(jb + 1) * BLOCK, :] = (y * lax.rsqrt(ms + EPS) * nw_ref[...]).astype(BF16)

    for t in range(n_items + 2):
        if t < n_items:
            logits(t)
        if 0 <= t - 1 < n_items:
            softmax(t - 1)
        if 0 <= t - 2 < n_items:
            weighted_sum(t - 2)


def _attention(q, kd, vd, layer, bias, sinks, nw, bsz, seq):
    tq = TQ_ATTN
    tiles = seq // tq
    ratio = tq // BLOCK

    def prev_map(b, i):
        return (jnp.maximum((b * tiles + i) * ratio - 1, 0), 0)

    return pl.pallas_call(
        functools.partial(_attn_kernel, layer=layer),
        out_shape=jax.ShapeDtypeStruct((bsz * seq, D_ATTN), BF16),
        grid=(bsz, tiles),
        in_specs=[
            pl.BlockSpec(memory_space=pltpu.SMEM),
            pl.BlockSpec((tq, D_ATTN), lambda b, i: (b * tiles + i, 0)),
            pl.BlockSpec((tq, 2 * KV_DIM), lambda b, i: (b * tiles + i, 0)),
            pl.BlockSpec((BLOCK, 2 * KV_DIM), prev_map),
            pl.BlockSpec((tq, 2 * KV_DIM), lambda b, i: (b * tiles + i, 0)),
            pl.BlockSpec((BLOCK, 2 * KV_DIM), prev_map),
            _const_spec((N_Q_HEADS, BLOCK, 2 * BLOCK)),
            _layer_spec((1, D_ATTN), layer),
        ],
        out_specs=pl.BlockSpec((tq, D_ATTN), lambda b, i: (b * tiles + i, 0)),
        scratch_shapes=[
            pltpu.VMEM((2, Q_PER_KV * BLOCK, 2 * BLOCK), F32),
            pltpu.VMEM((2, Q_PER_KV * BLOCK, 2 * BLOCK), BF16),
            pltpu.VMEM((2, Q_PER_KV * BLOCK, 2 * HEAD_DIM), F32),
            pltpu.VMEM((tq, D_ATTN), F32),
        ],
        compiler_params=_params(("arbitrary", "arbitrary")),
        name="swa_attention",
    )(sinks, q, kd, kd, vd, vd, bias, nw)


def _ffn_kernel(x_ref, ys_ref, ya_ref, mod_ref, wos_ref, woa_ref, n2w_ref, wv_ref, wg_ref,
                cwv_ref, cwg_ref, cbv_ref, cbg_ref, wd_ref, o_ref, h_ref, acc_ref, tail_ref,
                upv_ref, upg_ref, *, tiles_per_batch):
    first = (pl.program_id(0) % tiles_per_batch) == 0
    g1 = mod_ref[2:3, :]
    sh2 = mod_ref[3:4, :]
    sc2 = mod_ref[4:5, :]
    g2 = mod_ref[5:6, :]
    mixed = jnp.dot(ys_ref[...], wos_ref[...], preferred_element_type=F32)
    mixed = mixed + jnp.dot(ya_ref[...], woa_ref[...], preferred_element_type=F32)
    x1 = x_ref[...] + g1 * mixed
    ms = jnp.mean(x1 * x1, axis=-1, keepdims=True)
    h_ref[...] = ((x1 * lax.rsqrt(ms + EPS) * n2w_ref[...]) * (1.0 + sc2) + sh2).astype(BF16)
    acc_ref[...] = jnp.zeros_like(acc_ref)
    tm = x_ref.shape[0]

    def conv(up, cw, cb, tail_slot):
        prev = jnp.where(first, 0.0, tail_ref[tail_slot])
        tail_ref[tail_slot] = up[tm - SUBLANES:, :]
        ext = jnp.concatenate([prev, up], axis=0)
        up1 = pltpu.roll(ext, 1, axis=0)[SUBLANES:, :]
        up2 = pltpu.roll(ext, 2, axis=0)[SUBLANES:, :]
        return cw[0:1, :] * up2 + cw[1:2, :] * up1 + cw[2:3, :] * up + cb

    def up_proj(j, slot):
        h = h_ref[...]
        upv_ref[slot] = jnp.dot(h, wv_ref[j], preferred_element_type=F32)
        upg_ref[slot] = jnp.dot(h, wg_ref[j], preferred_element_type=F32)

    def consume(j, slot):
        val = conv(upv_ref[slot], cwv_ref[j], cbv_ref[j], 2 * j)
        gate = conv(upg_ref[slot], cwg_ref[j], cbg_ref[j], 2 * j + 1)
        act = (gate * (1.0 / (1.0 + jnp.exp(-gate))) * val).astype(BF16)
        acc_ref[...] += jnp.dot(act, wd_ref[j], preferred_element_type=F32)

    up_proj(0, 0)

    def two_chunks(jj, carry):
        j = 2 * jj
        up_proj(j + 1, 1)
        consume(j, 0)
        up_proj(j + 2, 0)
        consume(j + 1, 1)
        return carry

    lax.fori_loop(0, (N_FF_CHUNKS - 1) // 2, two_chunks, 0)
    consume(N_FF_CHUNKS - 1, 0)
    o_ref[...] = x1 + g2 * acc_ref[...]


def _ffn_weights(w_out, norm2_w, w_up, conv_w, conv_b, w_down):
    nc, tf = N_FF_CHUNKS, TF_FFN

    def cols(a):
        r = a.shape[1]
        return a.reshape(DEPTH, r, 2, nc, tf).transpose(0, 2, 3, 1, 4)

    return (w_out.astype(BF16).reshape(DEPTH, 2, D_MODEL // 2, D_MODEL),
            norm2_w.astype(F32).reshape(DEPTH, 1, D_MODEL),
            cols(w_up.astype(BF16)),
            cols(conv_w.astype(F32)),
            cols(conv_b.astype(F32).reshape(DEPTH, 1, 2 * D_FF)),
            w_down.astype(BF16).reshape(DEPTH, nc, tf, D_MODEL))


def _out_ffn(x2d, ys, ya, layer, mod, weights, tiles_per_batch):
    wo, n2w, wup, cw, cb, wd = weights
    ntok = x2d.shape[0]
    tm = TM_FFN
    nc, tf = N_FF_CHUNKS, TF_FFN
    return pl.pallas_call(
        functools.partial(_ffn_kernel, tiles_per_batch=tiles_per_batch),
        out_shape=jax.ShapeDtypeStruct((ntok, D_MODEL), F32),
        grid=(ntok // tm,),
        in_specs=[
            pl.BlockSpec((tm, D_MODEL), lambda i: (i, 0)),
            pl.BlockSpec((tm, D_SSM), lambda i: (i, 0)),
            pl.BlockSpec((tm, D_ATTN), lambda i: (i, 0)),
            pl.BlockSpec((None, None, N_MOD, D_MODEL), lambda i: (layer, i // tiles_per_batch, 0, 0)),
            _layer_spec((D_SSM, D_MODEL), layer, 0),
            _layer_spec((D_ATTN, D_MODEL), layer, 1),
            _layer_spec((1, D_MODEL), layer),
            _layer_spec((nc, D_MODEL, tf), layer, 0),
            _layer_spec((nc, D_MODEL, tf), layer, 1),
            _layer_spec((nc, 3, tf), layer, 0),
            _layer_spec((nc, 3, tf), layer, 1),
            _layer_spec((nc, 1, tf), layer, 0),
            _layer_spec((nc, 1, tf), layer, 1),
            _layer_spec((nc, tf, D_MODEL), layer),
        ],
        out_specs=pl.BlockSpec((tm, D_MODEL), lambda i: (i, 0)),
        scratch_shapes=[
            pltpu.VMEM((tm, D_MODEL), BF16),
            pltpu.VMEM((tm, D_MODEL), F32),
            pltpu.VMEM((2 * nc, SUBLANES, tf), F32),
            pltpu.VMEM((2, tm, tf), F32),
            pltpu.VMEM((2, tm, tf), F32),
        ],
        compiler_params=_params(("arbitrary",)),
        name="out_proj_convffn",
    )(x2d, ys, ya, mod, wo, wo, n2w, wup, wup, cw, cw, cb, cb, wd)


def _in_weights(w_in, qw, kw):
    def twice(w):
        w = w.reshape(DEPTH, D_MODEL, N_KV_HEADS, 1, HEAD_DIM)
        return jnp.broadcast_to(w, (DEPTH, D_MODEL, N_KV_HEADS, 2, HEAD_DIM)).reshape(DEPTH, D_MODEL, 2 * KV_DIM)

    k0 = D_SSM + D_ATTN
    w_ext = jnp.concatenate([w_in[:, :, :k0], twice(w_in[:, :, k0:k0 + KV_DIM]),
                             twice(w_in[:, :, k0 + KV_DIM:])], axis=2).astype(BF16)
    qw_ext = (jnp.tile(qw.astype(F32), (1, N_Q_HEADS)) * (HEAD_DIM ** -0.5 * LOG2E)).reshape(DEPTH, 1, D_ATTN)
    kw_ext = jnp.tile(kw.astype(F32), (1, 2 * N_KV_HEADS)).reshape(DEPTH, 1, 2 * KV_DIM)
    return w_ext, qw_ext, kw_ext


def kernel(x, c, w_mod, b_mod, norm1_w, w_in, lam_re, lam_im, log_dt, ssm_b_re, ssm_b_im, ssm_c_re,
           ssm_c_im, ssm_d, w_glu, b_glu, q_norm_w, k_norm_w, rel_bias, sinks, out_norm_ssm,
           out_norm_attn, w_out, norm2_w, w_up, conv_w, conv_b, w_down):
    bsz, seq, _ = x.shape
    assert seq % T_S5 == 0 and seq % TQ_ATTN == 0 and seq % TM_IN == 0 and seq % TM_FFN == 0
    mod = _modulation(c.astype(F32), w_mod.astype(F32), b_mod.astype(F32))
    mod = mod.reshape(DEPTH, bsz, N_MOD, D_MODEL)
    bias = _bias_table(rel_bias.astype(F32))
    n1w = norm1_w.astype(F32).reshape(DEPTH, 1, D_MODEL)
    w_ext, qw_ext, kw_ext = _in_weights(w_in, q_norm_w, k_norm_w)
    s5_ops = jax.vmap(_s5_operands)(lam_re, lam_im, log_dt, ssm_b_re, ssm_b_im, ssm_c_re, ssm_c_im)
    d_skip = ssm_d.astype(F32).reshape(DEPTH, 1, D_SSM)
    wglu = w_glu.astype(BF16)
    bglu = b_glu.astype(F32).reshape(DEPTH, 1, D_SSM)
    nw_ssm = out_norm_ssm.astype(F32).reshape(DEPTH, 1, D_SSM)
    nw_attn = out_norm_attn.astype(F32).reshape(DEPTH, 1, D_ATTN)
    sinks2 = sinks.astype(F32) * LOG2E
    ffn_w = _ffn_weights(w_out, norm2_w, w_up, conv_w, conv_b, w_down)
    x2d = x.reshape(bsz * seq, D_MODEL).astype(F32)
    for l in range(DEPTH):
        u, q, kd, vd = _in_projection(x2d, l, mod, n1w, w_ext, qw_ext, kw_ext, seq // TM_IN)
        ys = _s5(u, l, s5_ops, d_skip, wglu, bglu, nw_ssm, bsz, seq)
        ya = _attention(q, kd, vd, l, bias, sinks2, nw_attn, bsz, seq)
        x2d = _out_ffn(x2d, ys, ya, l, mod, ffn_w, seq // TM_FFN)
    return x2d.reshape(bsz, seq, D_MODEL).astype(x.dtype)
```
